```python
import math
import jax
import jax.numpy as jnp
from jax import lax
import numpy as np

D_MODEL = 1024
BATCH = 8
SEQ = 4096
DEPTH = 2

HEAD_DIM = 64
N_HEADS_DIFF = D_MODEL // (2 * HEAD_DIM)
DIFF_QK_DIM = HEAD_DIM // 2
N_HEADS_DIL = D_MODEL // (2 * HEAD_DIM)
N_ATTN_HEADS = N_HEADS_DIFF + N_HEADS_DIL
DIL_PAIRS = ((128, 1), (512, 4), (2048, 16))
ATTN_BLOCK = 128
SSM_GROUP_WIDTH = 16
SSM_GROUPS = D_MODEL // SSM_GROUP_WIDTH
SSM_STATE = 64
SSM_CHUNK = 128
D_FF = 2816
N_EXPERTS = 8
TOP_K = 2
D_FF_EXPERT = 1408
DN_ALPHA = (2.0 * DEPTH) ** 0.25
DN_BETA = (8.0 * DEPTH) ** -0.25
EPS = 1e-5
F32 = jnp.float32

kernel_name = 'hybrid_diff_dilated_s5_moe_block'


def _layernorm(x, g, b):
    xf = x.astype(F32)
    mu = jnp.mean(xf, axis=-1, keepdims=True)
    var = jnp.mean(jnp.square(xf - mu), axis=-1, keepdims=True)
    y = (xf - mu) * lax.rsqrt(var + EPS)
    return (y * g.astype(F32) + b.astype(F32)).astype(x.dtype)


def _rmsnorm(x, w):
    xf = x.astype(F32)
    y = xf * lax.rsqrt(jnp.mean(jnp.square(xf), axis=-1, keepdims=True) + EPS)
    return (y * w.astype(F32)).astype(x.dtype)


def _ada(c, w, b):
    mod = jax.nn.silu(c) @ w + b
    return jnp.split(mod[:, None, :], 6, axis=-1)


def _swiglu(h, w_gate, w_up, w_down):
    return (jax.nn.silu(h @ w_gate) * (h @ w_up)) @ w_down


def _alibi_slopes():
    i = jnp.arange(N_ATTN_HEADS, dtype=F32) + 1.0
    return jnp.exp2(-8.0 * i / N_ATTN_HEADS)


def _diff_attention(q, k, v, lam, slopes):
    bsz, n_h, s_len, _, dq = q.shape
    nb = s_len // ATTN_BLOCK
    qb = q.reshape(bsz, n_h, nb, ATTN_BLOCK, 2, dq).transpose(2, 0, 1, 3, 4, 5)
    scale = dq ** -0.5
    spos = jnp.arange(s_len)
    lam = lam.astype(F32)

    def block(args):
        i, qi = args
        s = jnp.einsum('bhqmd,bhkmd->bhmqk', qi, k).astype(F32) * scale
        dist = (i * ATTN_BLOCK + jnp.arange(ATTN_BLOCK))[:, None] - spos[None, :]
        bias = -slopes[:, None, None, None] * dist.astype(F32)
        s = jnp.where(dist >= 0, s + bias, -jnp.inf)
        p = jax.nn.softmax(s, axis=-1).astype(v.dtype)
        o = jnp.einsum('bhmqk,bhkd->bhmqd', p, v)
        return o[:, :, 0] - lam.astype(o.dtype) * o[:, :, 1]

    o = lax.map(block, (jnp.arange(nb), qb))
    return o.transpose(1, 2, 0, 3, 4).reshape(bsz, n_h, s_len, -1)


def _dilated_branch(q, k, v, slopes, window, dilation):
    bsz, n_h, s_len, dh = q.shape
    blk = ATTN_BLOCK
    n_back = window // dilation
    sub_len = -(-s_len // dilation)
    sub_len = -(-sub_len // blk) * blk
    s_pad = sub_len * dilation
    nb = sub_len // blk
    pad = ((0, 0), (0, 0), (0, s_pad - s_len), (0, 0))

    def strided(a):
        a = jnp.pad(a, pad).reshape(bsz, n_h, sub_len, dilation, dh).transpose(0, 1, 3, 2, 4)
        return a.reshape(bsz, n_h, dilation, nb, blk, dh)

    def with_prev(a):
        prev = jnp.pad(a, ((0, 0),) * 3 + ((1, 0), (0, 0), (0, 0)))[:, :, :, :-1]
        return jnp.concatenate([prev, a], axis=4)

    qs = strided(q)
    kc = with_prev(strided(k))
    vc = with_prev(strided(v))
    s = jnp.einsum('bhrnqd,bhrnkd->bhrnqk', qs, kc).astype(F32) * (dh ** -0.5)
    qi = np.arange(blk)[:, None]
    kj = np.arange(2 * blk)[None, :]
    dist = qi + blk - kj
    valid = (dist >= 0) & (dist <= n_back)
    mask = valid[None] & ((np.arange(nb)[:, None, None] > 0) | (kj[None] >= blk))
    bias = -slopes[:, None, None, None, None] * jnp.asarray(dist * dilation, F32)
    s = jnp.where(mask, s + bias, -jnp.inf)
    lse = jax.nn.logsumexp(s, axis=-1)
    p = jnp.exp(s - lse[..., None]).astype(v.dtype)
    o = jnp.einsum('bhrnqk,bhrnkd->bhrnqd', p, vc)
    o = o.reshape(bsz, n_h, dilation, sub_len, dh).transpose(0, 1, 3, 2, 4)
    o = o.reshape(bsz, n_h, s_pad, dh)[:, :, :s_len]
    lse = lse.reshape(bsz, n_h, dilation, sub_len).transpose(0, 1, 3, 2)
    lse = lse.reshape(bsz, n_h, s_pad)[:, :, :s_len]
    return o, lse


def _dilated_mixture(q, k, v, slopes):
    outs, lses = [], []
    for window, dilation in DIL_PAIRS:
        o, l = _dilated_branch(q, k, v, slopes, window, dilation)
        outs.append(o)
        lses.append(l)
    wts = jax.nn.softmax(jnp.stack(lses), axis=0).astype(q.dtype)
    return jnp.einsum('gbhs,gbhsd->bhsd', wts, jnp.stack(outs))


def _hybrid_attention(h, layer_idx, w_in, lam_q1, lam_k1, lam_q2, lam_k2, subln_w, w_out):
    bsz, s_len, _ = h.shape
    proj = h @ w_in
    aq, ak, av, bq, bk, bv = jnp.split(proj, 6, axis=-1)
    slopes = _alibi_slopes()
    q = aq.reshape(bsz, s_len, N_HEADS_DIFF, 2, DIFF_QK_DIM).transpose(0, 2, 1, 3, 4)
    k = ak.reshape(bsz, s_len, N_HEADS_DIFF, 2, DIFF_QK_DIM).transpose(0, 2, 1, 3, 4)
    v = av.reshape(bsz, s_len, N_HEADS_DIFF, HEAD_DIM).transpose(0, 2, 1, 3)
    lam_init = 0.8 - 0.6 * math.exp(-0.3 * layer_idx)
    lam = (jnp.exp(jnp.sum(lam_q1.astype(F32) * lam_k1.astype(F32)))
           - jnp.exp(jnp.sum(lam_q2.astype(F32) * lam_k2.astype(F32))) + lam_init)
    oa = _diff_attention(q, k, v, lam, slopes[:N_HEADS_DIFF])
    oa = _rmsnorm(oa, subln_w) * (1.0 - lam_init)
    q = bq.reshape(bsz, s_len, N_HEADS_DIL, HEAD_DIM).transpose(0, 2, 1, 3)
    k = bk.reshape(bsz, s_len, N_HEADS_DIL, HEAD_DIM).transpose(0, 2, 1, 3)
    v = bv.reshape(bsz, s_len, N_HEADS_DIL, HEAD_DIM).transpose(0, 2, 1, 3)
    ob = _dilated_mixture(q, k, v, slopes[N_HEADS_DIFF:])
    o = jnp.concatenate([oa, ob], axis=1).transpose(0, 2, 1, 3).reshape(bsz, s_len, D_MODEL)
    return o @ w_out


def _cplx_combine(e1, e2):
    a1r, a1i, b1r, b1i = e1
    a2r, a2i, b2r, b2i = e2
    return (a2r * a1r - a2i * a1i,
            a2r * a1i + a2i * a1r,
            a2r * b1r - a2i * b1i + b2r,
            a2r * b1i + a2i * b1r + b2i)


def _s5(h, a_re, a_im, log_dt, b_re, b_im, c_re, c_im, d_skip):
    bsz, s_len, _ = h.shape
    u = h.astype(F32)
    ar, ai = a_re.astype(F32), a_im.astype(F32)
    dt = jnp.exp(log_dt.astype(F32))[:, None]
    mag = jnp.exp(dt * ar)
    abr, abi = mag * jnp.cos(dt * ai), mag * jnp.sin(dt * ai)
    den = ar * ar + ai * ai
    nr, ni = abr - 1.0, abi
    zr, zi = (nr * ar + ni * ai) / den, (ni * ar - nr * ai) / den
    br, bi = b_re.astype(F32), b_im.astype(F32)
    bbr = zr[..., None] * br - zi[..., None] * bi
    bbi = zr[..., None] * bi + zi[..., None] * br
    cr, ci = c_re.astype(F32), c_im.astype(F32)
    n_chunks = s_len // SSM_CHUNK
    uc = u.reshape(bsz, n_chunks, SSM_CHUNK, SSM_GROUPS, SSM_GROUP_WIDTH).transpose(1, 2, 0, 3, 4)

    def chunk(carry, uk):
        h0r, h0i = carry
        xr = jnp.einsum('lbgc,gpc->lbgp', uk, bbr)
        xi = jnp.einsum('lbgc,gpc->lbgp', uk, bbi)
        a_r = jnp.broadcast_to(abr, xr.shape)
        a_i = jnp.broadcast_to(abi, xr.shape)
        cum_r, cum_i, hr, hi = lax.associative_scan(_cplx_combine, (a_r, a_i, xr, xi), axis=0)
        hr, hi = hr + cum_r * h0r - cum_i * h0i, hi + cum_r * h0i + cum_i * h0r
        y = jnp.einsum('lbgp,gcp->lbgc', hr, cr) - jnp.einsum('lbgp,gcp->lbgc', hi, ci)
        return (hr[-1], hi[-1]), y

    init = (jnp.zeros((bsz, SSM_GROUPS, SSM_STATE), F32),) * 2
    _, y = lax.scan(chunk, init, uc)
    y = y.transpose(2, 0, 1, 3, 4).reshape(bsz, s_len, D_MODEL)
    return (y + d_skip.astype(F32) * u).astype(h.dtype)


def _moe(h, router_w, router_b, w_gate, w_up, w_down):
    logits = (h @ router_w).astype(F32) + router_b.astype(F32)
    top_v, top_i = lax.top_k(logits, TOP_K)
    gates = jax.nn.softmax(top_v, axis=-1)
    cw = jnp.sum(jax.nn.one_hot(top_i, N_EXPERTS, dtype=F32) * gates[..., None], axis=-2)
    cw = cw.astype(h.dtype)
    y = jnp.zeros_like(h)
    for e in range(N_EXPERTS):
        y = y + cw[..., e:e + 1] * _swiglu(h, w_gate[e], w_up[e], w_down[e])
    return y


def _attention_layer(x, c, layer_idx, ada_w, ada_b, w_in, lam_q1, lam_k1, lam_q2, lam_k2,
                     subln_w, w_out, ln1_g, ln1_b, ffn_w_gate, ffn_w_up, ffn_w_down, ln2_g, ln2_b):
    sh1, sc1, g1, sh2, sc2, g2 = _ada(c, ada_w, ada_b)
    y = _hybrid_attention(x * (1.0 + sc1) + sh1, layer_idx, w_in, lam_q1, lam_k1, lam_q2, lam_k2,
                          subln_w, w_out)
    x = _layernorm(DN_ALPHA * x + (1.0 + g1) * y, ln1_g, ln1_b)
    y = _swiglu(x * (1.0 + sc2) + sh2, ffn_w_gate, ffn_w_up, ffn_w_down)
    return _layernorm(DN_ALPHA * x + (1.0 + g2) * y, ln2_g, ln2_b)


def _ssm_layer(x, c, ada_w, ada_b, a_re, a_im, log_dt, b_re, b_im, c_re, c_im, d_skip, w_glu, b_glu,
               ln1_g, ln1_b, router_w, router_b, exp_w_gate, exp_w_up, exp_w_down, ln2_g, ln2_b):
    sh1, sc1, g1, sh2, sc2, g2 = _ada(c, ada_w, ada_b)
    y = _s5(x * (1.0 + sc1) + sh1, a_re, a_im, log_dt, b_re, b_im, c_re, c_im, d_skip)
    z = jax.nn.gelu(y) @ w_glu + b_glu
    y = z[..., :D_MODEL] * jax.nn.sigmoid(z[..., D_MODEL:])
    x = _layernorm(DN_ALPHA * x + (1.0 + g1) * y, ln1_g, ln1_b)
    y = _moe(x * (1.0 + sc2) + sh2, router_w, router_b, exp_w_gate, exp_w_up, exp_w_down)
    return _layernorm(DN_ALPHA * x + (1.0 + g2) * y, ln2_g, ln2_b)


def setup_inputs(seed: int = 0) -> dict:
    key = jax.random.key(seed)
    ks = iter(jax.random.split(key, 64))
    D = D_MODEL

    def nrm(shape, scale):
        return jax.random.normal(next(ks), shape, F32) * scale

    def gain(n):
        return 1.0 + nrm((n,), 0.01)

    ada_scale = 0.1 * D ** -0.5
    v_cols = jnp.concatenate([jnp.ones((D,), F32), jnp.full((D // 2,), DN_BETA, F32),
                              jnp.ones((D,), F32), jnp.full((D // 2,), DN_BETA, F32)])
    n_idx = jnp.arange(SSM_STATE, dtype=F32)[None, :]
    p = {}
    p['x'] = nrm((BATCH, SEQ, D), 1.0)
    p['c'] = nrm((BATCH, D), 1.0)
    p['l0_ada_w'] = nrm((D, 6 * D), ada_scale)
    p['l0_ada_b'] = nrm((6 * D,), 0.01)
    p['l0_w_in'] = nrm((D, 3 * D), D ** -0.5) * v_cols
    p['l0_lam_q1'] = nrm((DIFF_QK_DIM,), 0.1)
    p['l0_lam_k1'] = nrm((DIFF_QK_DIM,), 0.1)
    p['l0_lam_q2'] = nrm((DIFF_QK_DIM,), 0.1)
    p['l0_lam_k2'] = nrm((DIFF_QK_DIM,), 0.1)
    p['l0_subln_w'] = gain(HEAD_DIM)
    p['l0_w_out'] = nrm((D, D), DN_BETA * D ** -0.5)
    p['l0_ln1_g'] = gain(D)
    p['l0_ln1_b'] = nrm((D,), 0.01)
    p['l0_ffn_w_gate'] = nrm((D, D_FF), D ** -0.5)
    p['l0_ffn_w_up'] = nrm((D, D_FF), D ** -0.5)
    p['l0_ffn_w_down'] = nrm((D_FF, D), DN_BETA * D_FF ** -0.5)
    p['l0_ln2_g'] = gain(D)
    p['l0_ln2_b'] = nrm((D,), 0.01)
    p['l1_ada_w'] = nrm((D, 6 * D), ada_scale)
    p['l1_ada_b'] = nrm((6 * D,), 0.01)
    p['l1_a_re'] = -0.5 + nrm((SSM_GROUPS, SSM_STATE), 0.01)
    p['l1_a_im'] = math.pi * n_idx + nrm((SSM_GROUPS, SSM_STATE), 0.01)
    p['l1_log_dt'] = jax.random.uniform(next(ks), (SSM_GROUPS,), F32, math.log(1e-3), math.log(1e-1))
    p['l1_b_re'] = nrm((SSM_GROUPS, SSM_STATE, SSM_GROUP_WIDTH), (2.0 * SSM_GROUP_WIDTH) ** -0.5)
    p['l1_b_im'] = nrm((SSM_GROUPS, SSM_STATE, SSM_GROUP_WIDTH), (2.0 * SSM_GROUP_WIDTH) ** -0.5)
    p['l1_c_re'] = nrm((SSM_GROUPS, SSM_GROUP_WIDTH, SSM_STATE), (2.0 * SSM_STATE) ** -0.5)
    p['l1_c_im'] = nrm((SSM_GROUPS, SSM_GROUP_WIDTH, SSM_STATE), (2.0 * SSM_STATE) ** -0.5)
    p['l1_d_skip'] = nrm((D,), 0.5)
    p['l1_w_glu'] = nrm((D, 2 * D), DN_BETA * D ** -0.5)
    p['l1_b_glu'] = nrm((2 * D,), 0.01)
    p['l1_ln1_g'] = gain(D)
    p['l1_ln1_b'] = nrm((D,), 0.01)
    p['l1_router_w'] = nrm((D, N_EXPERTS), D ** -0.5)
    p['l1_router_b'] = nrm((N_EXPERTS,), 0.01)
    p['l1_exp_w_gate'] = nrm((N_EXPERTS, D, D_FF_EXPERT), D ** -0.5)
    p['l1_exp_w_up'] = nrm((N_EXPERTS, D, D_FF_EXPERT), D ** -0.5)
    p['l1_exp_w_down'] = nrm((N_EXPERTS, D_FF_EXPERT, D), DN_BETA * D_FF_EXPERT ** -0.5)
    p['l1_ln2_g'] = gain(D)
    p['l1_ln2_b'] = nrm((D,), 0.01)
    return p


def reference(x, c,
              l0_ada_w, l0_ada_b, l0_w_in, l0_lam_q1, l0_lam_k1, l0_lam_q2, l0_lam_k2, l0_subln_w,
              l0_w_out, l0_ln1_g, l0_ln1_b, l0_ffn_w_gate, l0_ffn_w_up, l0_ffn_w_down, l0_ln2_g, l0_ln2_b,
              l1_ada_w, l1_ada_b, l1_a_re, l1_a_im, l1_log_dt, l1_b_re, l1_b_im, l1_c_re, l1_c_im,
              l1_d_skip, l1_w_glu, l1_b_glu, l1_ln1_g, l1_ln1_b, l1_router_w, l1_router_b,
              l1_exp_w_gate, l1_exp_w_up, l1_exp_w_down, l1_ln2_g, l1_ln2_b):
    layer_params = (
        (l0_ada_w, l0_ada_b, l0_w_in, l0_lam_q1, l0_lam_k1, l0_lam_q2, l0_lam_k2, l0_subln_w,
         l0_w_out, l0_ln1_g, l0_ln1_b, l0_ffn_w_gate, l0_ffn_w_up, l0_ffn_w_down, l0_ln2_g, l0_ln2_b),
        (l1_ada_w, l1_ada_b, l1_a_re, l1_a_im, l1_log_dt, l1_b_re, l1_b_im, l1_c_re, l1_c_im,
         l1_d_skip, l1_w_glu, l1_b_glu, l1_ln1_g, l1_ln1_b, l1_router_w, l1_router_b,
         l1_exp_w_gate, l1_exp_w_up, l1_exp_w_down, l1_ln2_g, l1_ln2_b),
    )
    for i in range(DEPTH):
        if i % 2 == 0:
            x = _attention_layer(x, c, i, *layer_params[i])
        else:
            x = _ssm_layer(x, c, *layer_params[i])
    return x
```

```python
import functools
import math

import jax
import jax.numpy as jnp
from jax import lax
from jax.experimental import pallas as pl
from jax.experimental.pallas import tpu as pltpu

F32 = jnp.float32
BF16 = jnp.bfloat16

D_MODEL = 1024
DEPTH = 2
HEAD_DIM = 64
N_HEADS_DIFF = D_MODEL // (2 * HEAD_DIM)
DIFF_QK_DIM = HEAD_DIM // 2
N_HEADS_DIL = D_MODEL // (2 * HEAD_DIM)
N_ATTN_HEADS = N_HEADS_DIFF + N_HEADS_DIL
DIL_PAIRS = ((128, 1), (512, 4), (2048, 16))
ATTN_BLOCK = 128
SSM_GROUP_WIDTH = 16
SSM_GROUPS = D_MODEL // SSM_GROUP_WIDTH
SSM_STATE = 64
SSM_SUB = 16
D_FF = 2816
N_EXPERTS = 8
D_FF_EXPERT = 1408
DN_ALPHA = (2.0 * DEPTH) ** 0.25
EPS = 1e-5
LOG2E = 1.4426950408889634
LANES = 128
HALF_D = D_MODEL // 2
NEG_INF = float("-inf")

_NT = (((1,), (1,)), ((), ()))
_HI = lax.Precision.HIGHEST


def _params(sem, vmem_mb=48):
    return pltpu.CompilerParams(dimension_semantics=sem, vmem_limit_bytes=vmem_mb * 1024 * 1024)


def _layernorm(z, g, b):
    mu = jnp.mean(z, axis=-1, keepdims=True)
    zc = z - mu
    var = jnp.mean(zc * zc, axis=-1, keepdims=True)
    return zc * lax.rsqrt(var + EPS) * g + b


def _ada_kernel(c_ref, w_ref, b_ref, o_ref):
    h = jax.nn.silu(c_ref[...])
    o_ref[...] = jnp.dot(h.astype(BF16), w_ref[...].astype(BF16),
                         preferred_element_type=F32) + b_ref[...]


def _ada(c, w, b):
    bsz = c.shape[0]
    n = w.shape[1]
    tn = 1024
    out = pl.pallas_call(
        _ada_kernel,
        grid=(n // tn,),
        in_specs=[pl.BlockSpec((bsz, D_MODEL), lambda j: (0, 0)),
                  pl.BlockSpec((D_MODEL, tn), lambda j: (0, j)),
                  pl.BlockSpec((1, tn), lambda j: (0, j))],
        out_specs=pl.BlockSpec((bsz, tn), lambda j: (0, j)),
        out_shape=jax.ShapeDtypeStruct((bsz, n), F32),
        compiler_params=_params(("arbitrary",)),
        name="ada",
    )(c, w, b.reshape(1, n))
    return out.reshape(bsz, 6, D_MODEL)


def _qkv_kernel(x_ref, mod_ref, w_ref, cs_ref, o_ref):
    mod = mod_ref[0]
    h = (x_ref[0] * (1.0 + mod[1:2]) + mod[0:1]).astype(BF16)
    for j in range(6):
        sl = slice(j * HALF_D, (j + 1) * HALF_D)
        acc = jnp.dot(h, w_ref[:, sl], preferred_element_type=F32)
        o_ref[0, :, sl] = (acc * cs_ref[:, sl]).astype(BF16)


def _qkv(x, mod, w, colscale, tm=512):
    bsz, s_len, _ = x.shape
    n = w.shape[1]
    return pl.pallas_call(
        _qkv_kernel,
        grid=(bsz, s_len // tm),
        in_specs=[pl.BlockSpec((1, tm, D_MODEL), lambda b, i: (b, i, 0)),
                  pl.BlockSpec((1, 6, D_MODEL), lambda b, i: (b, 0, 0)),
                  pl.BlockSpec((D_MODEL, n), lambda b, i: (0, 0)),
                  pl.BlockSpec((1, n), lambda b, i: (0, 0))],
        out_specs=pl.BlockSpec((1, tm, n), lambda b, i: (b, i, 0)),
        out_shape=jax.ShapeDtypeStruct((bsz, s_len, n), BF16),
        compiler_params=_params(("parallel", "parallel")),
        name="qkv",
    )(x, mod, w, colscale)


def _diff_kernel(slope_ref, lam_ref, subw_ref, q_ref, k_ref, v_ref, o_ref,
                 acc_ref, m_ref, bias_ref, *, t_blk, lam_init):
    pr = pl.program_id(1)
    qi = pl.program_id(2)
    lane = lax.broadcasted_iota(jnp.int32, (1, LANES), 1)
    sl2 = [slope_ref[2 * pr + hh] * LOG2E for hh in range(2)]

    @pl.when(qi == 0)
    def _():
        ri = lax.broadcasted_iota(jnp.int32, (t_blk, t_blk), 0)
        cj = lax.broadcasted_iota(jnp.int32, (t_blk, t_blk), 1)
        dist = (ri - cj).astype(F32)
        for hh in range(2):
            full = -sl2[hh] * dist
            bias_ref[hh, 0] = full
            bias_ref[hh, 1] = jnp.where(ri >= cj, full, NEG_INF)

    q = q_ref[0]
    qm = [jnp.where(lane // DIFF_QK_DIM == m, q, jnp.zeros_like(q)) for m in range(4)]
    acc_ref[...] = jnp.zeros_like(acc_ref)
    m_ref[...] = jnp.full_like(m_ref, NEG_INF)
    one = jnp.ones((), BF16)

    def kv_step(kv, diag):
        off = pl.multiple_of(kv * t_blk, t_blk)
        k = k_ref[0, pl.ds(off, t_blk), :]
        v = v_ref[0, pl.ds(off, t_blk), :]
        va = [jnp.where(lane < HEAD_DIM, v, one), jnp.where(lane >= HEAD_DIM, v, one)]
        blk_off = ((qi - kv) * t_blk).astype(F32)
        for m in range(4):
            hh = m // 2
            s = lax.dot_general(qm[m], k, _NT, preferred_element_type=F32)
            t = s + bias_ref[hh, 1 if diag else 0]
            c = -sl2[hh] * blk_off
            m_old = m_ref[m]
            m_new = jnp.maximum(m_old, jnp.max(t, axis=-1, keepdims=True) + c)
            alpha = jnp.exp2(m_old - m_new)
            p = jnp.exp2(t - (m_new - c))
            acc_ref[m] = alpha * acc_ref[m] + jnp.dot(p.astype(BF16), va[hh],
                                                     preferred_element_type=F32)
            m_ref[m] = m_new

    def body(kv, carry):
        kv_step(kv, False)
        return carry

    lax.fori_loop(0, qi, body, 0)
    kv_step(qi, True)

    lam_v = lam_ref[...]
    lam = (jnp.exp(jnp.sum(lam_v[0:1] * lam_v[1:2], axis=-1, keepdims=True))
           - jnp.exp(jnp.sum(lam_v[2:3] * lam_v[3:4], axis=-1, keepdims=True)) + lam_init)
    d_heads = []
    for hh in range(2):
        outs = []
        for mm in range(2):
            a = acc_ref[2 * hh + mm]
            outs.append(a / pltpu.roll(a, HEAD_DIM, 1))
        d_heads.append(outs[0] - lam * outs[1])
    even = lane < HEAD_DIM
    dp = jnp.where(even, d_heads[0], d_heads[1])
    sq = dp * dp
    s_even = jnp.sum(jnp.where(even, sq, 0.0), axis=-1, keepdims=True)
    s_odd = jnp.sum(jnp.where(even, 0.0, sq), axis=-1, keepdims=True)
    ms = jnp.where(even, s_even, s_odd) * (1.0 / HEAD_DIM)
    y = dp * lax.rsqrt(ms + EPS) * subw_ref[...]
    o_ref[0] = (y * (1.0 - lam_init)).astype(BF16)


def _diff_attention(proj, slopes, lam_vecs, subw, layer_idx, t_blk=256):
    bsz, s_len, _ = proj.shape
    n_pairs = N_HEADS_DIFF // 2
    lam_init = 0.8 - 0.6 * math.exp(-0.3 * layer_idx)
    kern = functools.partial(_diff_kernel, t_blk=t_blk, lam_init=lam_init)
    return pl.pallas_call(
        kern,
        grid=(bsz, n_pairs, s_len // t_blk),
        in_specs=[pl.BlockSpec(memory_space=pltpu.SMEM),
                  pl.BlockSpec((4, DIFF_QK_DIM), lambda b, p, i: (0, 0)),
                  pl.BlockSpec((1, LANES), lambda b, p, i: (0, 0)),
                  pl.BlockSpec((1, t_blk, LANES), lambda b, p, i: (b, i, p)),
                  pl.BlockSpec((1, s_len, LANES), lambda b, p, i: (b, 0, n_pairs + p)),
                  pl.BlockSpec((1, s_len, LANES), lambda b, p, i: (b, 0, 2 * n_pairs + p))],
        out_specs=pl.BlockSpec((1, t_blk, LANES), lambda b, p, i: (b, i, p)),
        out_shape=jax.ShapeDtypeStruct((bsz, s_len, HALF_D), BF16),
        scratch_shapes=[pltpu.VMEM((4, t_blk, LANES), F32),
                        pltpu.VMEM((4, t_blk, 1), F32),
                        pltpu.VMEM((2, 2, t_blk, t_blk), F32)],
        compiler_params=_params(("parallel", "parallel", "arbitrary")),
        name="diff_attn",
    )(slopes, lam_vecs, subw, proj, proj, proj)


def _dil_kernel(q_ref, kp_ref, kc_ref, vp_ref, vc_ref, o_ref, l_ref, bias_ref, *, dil, slopes):
    blk = ATTN_BLOCK
    n = pl.program_id(2)
    first = (pl.program_id(0) == 0) & (pl.program_id(1) == 0) & (n == 0)

    @pl.when(first)
    def _():
        qi = lax.broadcasted_iota(jnp.int32, (blk, 2 * blk), 0)
        kj = lax.broadcasted_iota(jnp.int32, (blk, 2 * blk), 1)
        dist = qi + blk - kj
        valid = (dist >= 0) & (dist <= blk)
        dist_f = (dist * dil).astype(F32)
        for h in range(N_HEADS_DIL):
            bias = (-slopes[h] * LOG2E) * dist_f
            bias_ref[h, 0] = jnp.where(valid, bias, NEG_INF)
            bias_ref[h, 1] = jnp.where(valid & (kj >= blk), bias, NEG_INF)

    sel = (n == 0).astype(jnp.int32)
    lane = lax.broadcasted_iota(jnp.int32, (1, LANES), 1)
    even = lane < HEAD_DIM
    one = jnp.ones((), BF16)
    for p in range(N_HEADS_DIL // 2):
        sl = slice(p * LANES, (p + 1) * LANES)
        qp = q_ref[0, :, sl]
        kcat = jnp.concatenate([kp_ref[0, :, sl], kc_ref[0, :, sl]], axis=0)
        vcat = jnp.concatenate([vp_ref[0, :, sl], vc_ref[0, :, sl]], axis=0)
        accs, ms = [], []
        for hh in range(2):
            keep = even if hh == 0 else jnp.logical_not(even)
            qh = jnp.where(keep, qp, jnp.zeros_like(qp))
            va = jnp.where(keep, vcat, one)
            s = lax.dot_general(qh, kcat, _NT, preferred_element_type=F32)
            t = s + bias_ref[2 * p + hh, sel]
            m = jnp.max(t, axis=-1, keepdims=True)
            pm = jnp.exp2(t - m)
            accs.append(jnp.dot(pm.astype(BF16), va, preferred_element_type=F32))
            ms.append(m)
        o_un = jnp.where(even, accs[0], accs[1])
        l_sum = pltpu.roll(jnp.where(even, accs[1], accs[0]), HEAD_DIM, 1)
        m_pair = jnp.where(even, ms[0], ms[1])
        o_ref[0, :, sl] = o_un / l_sum
        l_ref[0, :, sl] = m_pair + jnp.log2(l_sum)


def _dilated_branch(proj, dil, slopes):
    bsz, s_len, width = proj.shape
    blk = ATTN_BLOCK
    sub_len = s_len // dil
    nb = sub_len // blk
    pv = proj.reshape(bsz, sub_len, dil * width)
    ncol = width // HALF_D
    kern = functools.partial(_dil_kernel, dil=dil, slopes=slopes)

    def cur(col):
        return pl.BlockSpec((1, blk, HALF_D), lambda b, r, n: (b, n, r * ncol + col))

    def prev(col):
        return pl.BlockSpec((1, blk, HALF_D), lambda b, r, n: (b, jnp.maximum(n - 1, 0), r * ncol + col))

    out_spec = pl.BlockSpec((1, blk, HALF_D), lambda b, r, n: (b, n, r))
    o, lse = pl.pallas_call(
        kern,
        grid=(bsz, dil, nb),
        in_specs=[cur(3), prev(4), cur(4), prev(5), cur(5)],
        out_specs=[out_spec, out_spec],
        out_shape=[jax.ShapeDtypeStruct((bsz, sub_len, dil * HALF_D), F32)] * 2,
        scratch_shapes=[pltpu.VMEM((N_HEADS_DIL, 2, blk, 2 * blk), F32)],
        compiler_params=_params(("arbitrary", "arbitrary", "arbitrary")),
        name=f"dilated_d{dil}",
    )(pv, pv, pv, pv, pv)
    return o.reshape(bsz, s_len, HALF_D), lse.reshape(bsz, s_len, HALF_D)


def _attn_out_kernel(oa_ref, o1_ref, l1_ref, o2_ref, l2_ref, o3_ref, l3_ref, x_ref, mod_ref,
                     w_ref, g_ref, b_ref, out_ref):
    l1, l2, l3 = l1_ref[0], l2_ref[0], l3_ref[0]
    mx = jnp.maximum(jnp.maximum(l1, l2), l3)
    e1, e2, e3 = jnp.exp2(l1 - mx), jnp.exp2(l2 - mx), jnp.exp2(l3 - mx)
    ob = (e1 * o1_ref[0] + e2 * o2_ref[0] + e3 * o3_ref[0]) / (e1 + e2 + e3)
    y = (jnp.dot(oa_ref[0], w_ref[:HALF_D], preferred_element_type=F32)
         + jnp.dot(ob.astype(BF16), w_ref[HALF_D:], preferred_element_type=F32))
    mod = mod_ref[0]
    z = DN_ALPHA * x_ref[0] + (1.0 + mod[2:3]) * y
    out_ref[0] = _layernorm(z, g_ref[...], b_ref[...])


def _attn_out(oa, dil_outs, x, mod, w, ln_g, ln_b, tm=256):
    bsz, s_len, _ = x.shape
    half = pl.BlockSpec((1, tm, HALF_D), lambda b, i: (b, i, 0))
    full = pl.BlockSpec((1, tm, D_MODEL), lambda b, i: (b, i, 0))
    vec = pl.BlockSpec((1, D_MODEL), lambda b, i: (0, 0))
    flat = [a for pair in dil_outs for a in pair]
    return pl.pallas_call(
        _attn_out_kernel,
        grid=(bsz, s_len // tm),
        in_specs=[half] * 7 + [full, pl.BlockSpec((1, 6, D_MODEL), lambda b, i: (b, 0, 0)),
                               pl.BlockSpec((D_MODEL, D_MODEL), lambda b, i: (0, 0)), vec, vec],
        out_specs=full,
        out_shape=jax.ShapeDtypeStruct((bsz, s_len, D_MODEL), F32),
        compiler_params=_params(("parallel", "parallel")),
        name="attn_out",
    )(oa, *flat, x, mod, w, ln_g.reshape(1, -1), ln_b.reshape(1, -1))


def _ffn_kernel(*refs, gated, emit_next):
    it = iter(refs)
    x_ref, mod_ref = next(it), next(it)
    cw_ref = next(it) if gated else None
    wg_ref, wu_ref, wd_ref, g_ref, b_ref = next(it), next(it), next(it), next(it), next(it)
    nmod_ref = next(it) if emit_next else None
    out_ref = next(it)
    nxt_ref = next(it) if emit_next else None
    h_ref, acc_ref = next(it), next(it)

    j = pl.program_id(2)
    mod = mod_ref[0]

    @pl.when(j == 0)
    def _():
        h_ref[...] = (x_ref[0] * (1.0 + mod[4:5]) + mod[3:4]).astype(BF16)
        acc_ref[...] = jnp.zeros_like(acc_ref)

    h = h_ref[...]
    wg = wg_ref[0] if gated else wg_ref[...]
    wu = wu_ref[0] if gated else wu_ref[...]
    wd = wd_ref[0] if gated else wd_ref[...]
    gate = jnp.dot(h, wg, preferred_element_type=F32)
    up = jnp.dot(h, wu, preferred_element_type=F32)
    act = (jax.nn.silu(gate) * up).astype(BF16)
    y = jnp.dot(act, wd, preferred_element_type=F32)
    if gated:
        lane = lax.broadcasted_iota(jnp.int32, (1, LANES), 1)
        cw = jnp.sum(jnp.where(lane == j, cw_ref[0], 0.0), axis=-1, keepdims=True)
        y = cw * y
    acc_ref[...] += y

    @pl.when(j == pl.num_programs(2) - 1)
    def _():
        z = DN_ALPHA * x_ref[0] + (1.0 + mod[5:6]) * acc_ref[...]
        out = _layernorm(z, g_ref[...], b_ref[...])
        out_ref[0] = out
        if emit_next:
            nmod = nmod_ref[0]
            nxt_ref[0] = (out * (1.0 + nmod[1:2]) + nmod[0:1]).astype(BF16)


def _ffn(x, mod, wg, wu, wd, ln_g, ln_b, *, cw=None, next_mod=None, tm=512, tf=1408):
    bsz, s_len, _ = x.shape
    gated = cw is not None
    emit_next = next_mod is not None
    full = pl.BlockSpec((1, tm, D_MODEL), lambda b, i, j: (b, i, 0))
    modspec = pl.BlockSpec((1, 6, D_MODEL), lambda b, i, j: (b, 0, 0))
    vec = pl.BlockSpec((1, D_MODEL), lambda b, i, j: (0, 0))
    if gated:
        n_inner = wg.shape[0]
        w_in_spec = pl.BlockSpec((1, D_MODEL, wg.shape[2]), lambda b, i, j: (j, 0, 0))
        w_dn_spec = pl.BlockSpec((1, wd.shape[1], D_MODEL), lambda b, i, j: (j, 0, 0))
    else:
        n_inner = wg.shape[1] // tf
        w_in_spec = pl.BlockSpec((D_MODEL, tf), lambda b, i, j: (0, j))
        w_dn_spec = pl.BlockSpec((tf, D_MODEL), lambda b, i, j: (j, 0))
    in_specs = [full, modspec]
    args = [x, mod]
    if gated:
        in_specs.append(pl.BlockSpec((1, tm, LANES), lambda b, i, j: (b, i, 0)))
        args.append(cw)
    in_specs += [w_in_spec, w_in_spec, w_dn_spec, vec, vec]
    args += [wg, wu, wd, ln_g.reshape(1, -1), ln_b.reshape(1, -1)]
    out_specs = [full]
    out_shape = [jax.ShapeDtypeStruct((bsz, s_len, D_MODEL), F32)]
    if emit_next:
        in_specs.append(modspec)
        args.append(next_mod)
        out_specs.append(full)
        out_shape.append(jax.ShapeDtypeStruct((bsz, s_len, D_MODEL), BF16))
    res = pl.pallas_call(
        functools.partial(_ffn_kernel, gated=gated, emit_next=emit_next),
        grid=(bsz, s_len // tm, n_inner),
        in_specs=in_specs,
        out_specs=out_specs,
        out_shape=out_shape,
        scratch_shapes=[pltpu.VMEM((tm, D_MODEL), BF16), pltpu.VMEM((tm, D_MODEL), F32)],
        compiler_params=_params(("parallel", "parallel", "arbitrary")),
        name="moe_ffn" if gated else "ffn",
    )(*args)
    return res if emit_next else res[0]


def _s5_kernel(ar_r_ref, ai_r_ref, ar_c_ref, ai_c_ref, ldt_ref, brep_re_ref, brep_im_ref,
               bt_re_ref, bt_im_ref, c_re_ref, c_im_ref, u_ref, y_ref,
               sx_re, sx_im, h_re, h_im):
    sub = SSM_SUB
    gw = SSM_GROUP_WIDTH
    n_lane = sub * gw
    dt = jnp.exp(ldt_ref[0])

    def discretise(ar, ai):
        mag = jnp.exp(dt * ar)
        abr, abi = mag * jnp.cos(dt * ai), mag * jnp.sin(dt * ai)
        den = ar * ar + ai * ai
        nr, ni = abr - 1.0, abi
        return abr, abi, (nr * ar + ni * ai) / den, (ni * ar - nr * ai) / den

    abr, abi, zr, zi = discretise(ar_r_ref[0], ai_r_ref[0])
    _, _, zrc, zic = discretise(ar_c_ref[0], ai_c_ref[0])
    bt_r, bt_i = bt_re_ref[0], bt_im_ref[0]
    bbt_r, bbt_i = zr * bt_r - zi * bt_i, zr * bt_i + zi * bt_r
    brep_r, brep_i = brep_re_ref[0], brep_im_ref[0]
    bbrep_r, bbrep_i = zrc * brep_r - zic * brep_i, zrc * brep_i + zic * brep_r

    pw = []
    pr, pi = jnp.ones_like(abr), jnp.zeros_like(abr)
    for _ in range(sub + 1):
        pw.append((pr, pi))
        pr, pi = pr * abr - pi * abi, pr * abi + pi * abr
    cr, ci = c_re_ref[0], c_im_ref[0]
    m_re = [cr * a - ci * b for a, b in pw]
    m_im = [cr * b + ci * a for a, b in pw]

    kk = (jnp.dot(jnp.concatenate(m_re[:sub], axis=0), bbrep_r, precision=_HI, preferred_element_type=F32)
          - jnp.dot(jnp.concatenate(m_im[:sub], axis=0), bbrep_i, precision=_HI, preferred_element_type=F32))
    lane_slot = lax.broadcasted_iota(jnp.int32, (n_lane, n_lane), 1) // gw
    tt = jnp.zeros((n_lane, n_lane), F32)
    for s in range(sub):
        if s == 0:
            shifted = kk
        else:
            shifted = jnp.concatenate([jnp.zeros((s * gw, n_lane), F32), kk[:n_lane - s * gw]], axis=0)
        tt = jnp.where(lane_slot == s, shifted, tt)

    w_st_r = jnp.concatenate([bbt_r * pw[sub - 1 - s][0] - bbt_i * pw[sub - 1 - s][1] for s in range(sub)], axis=0)
    w_st_i = jnp.concatenate([bbt_r * pw[sub - 1 - s][1] + bbt_i * pw[sub - 1 - s][0] for s in range(sub)], axis=0)
    w_out_r = jnp.concatenate(m_re[1:], axis=0)
    w_out_i = jnp.concatenate(m_im[1:], axis=0)

    u = u_ref[0]
    sx_re[...] = jnp.dot(u, w_st_r.astype(BF16), preferred_element_type=F32)
    sx_im[...] = jnp.dot(u, w_st_i.astype(BF16), preferred_element_type=F32)

    n_batch = 8
    alr = jnp.broadcast_to(pw[sub][0], (n_batch, SSM_STATE))
    ali = jnp.broadcast_to(pw[sub][1], (n_batch, SSM_STATE))

    def body(j, carry):
        hr, hi = carry
        off = pl.multiple_of(j * n_batch, n_batch)
        h_re[pl.ds(off, n_batch), :] = hr
        h_im[pl.ds(off, n_batch), :] = hi
        sr = sx_re[pl.ds(off, n_batch), :]
        si = sx_im[pl.ds(off, n_batch), :]
        return alr * hr - ali * hi + sr, alr * hi + ali * hr + si

    zero = jnp.zeros((n_batch, SSM_STATE), F32)
    lax.fori_loop(0, u.shape[0] // n_batch, body, (zero, zero))

    y = lax.dot_general(u, tt.astype(BF16), _NT, preferred_element_type=F32)
    y += lax.dot_general(h_re[...].astype(BF16), w_out_r.astype(BF16), _NT, preferred_element_type=F32)
    y -= lax.dot_general(h_im[...].astype(BF16), w_out_i.astype(BF16), _NT, preferred_element_type=F32)
    y_ref[0] = y


def _s5(u, a_re, a_im, log_dt, b_re, b_im, c_re, c_im):
    bsz, s_len, _ = u.shape
    assert bsz == 8, "rows of the SSM layout hold one batch per sublane"
    g, p, gw, sub = SSM_GROUPS, SSM_STATE, SSM_GROUP_WIDTH, SSM_SUB
    nj = s_len // sub
    rows = nj * bsz
    ur = u.reshape(bsz, nj, sub, g, gw).transpose(3, 1, 0, 2, 4).reshape(g, rows, sub * gw)
    grp = lambda *shape: pl.BlockSpec((1,) + shape, lambda i: (i,) + (0,) * len(shape))
    yr = pl.pallas_call(
        _s5_kernel,
        grid=(g,),
        in_specs=[grp(1, p), grp(1, p), grp(p, 1), grp(p, 1), grp(1, 1),
                  grp(p, sub * gw), grp(p, sub * gw), grp(gw, p), grp(gw, p), grp(gw, p), grp(gw, p),
                  grp(rows, sub * gw)],
        out_specs=grp(rows, sub * gw),
        out_shape=jax.ShapeDtypeStruct((g, rows, sub * gw), F32),
        scratch_shapes=[pltpu.VMEM((rows, p), F32)] * 4,
        compiler_params=_params(("parallel",)),
        name="s5",
    )(a_re.reshape(g, 1, p), a_im.reshape(g, 1, p), a_re.reshape(g, p, 1), a_im.reshape(g, p, 1),
      log_dt.reshape(g, 1, 1), jnp.tile(b_re, (1, 1, sub)), jnp.tile(b_im, (1, 1, sub)),
      jnp.swapaxes(b_re, 1, 2), jnp.swapaxes(b_im, 1, 2), c_re, c_im, ur)
    return yr.reshape(g, nj, bsz, sub, gw).transpose(2, 1, 3, 0, 4).reshape(bsz, s_len, D_MODEL)


def _glu_kernel(x_ref, y_ref, mod_ref, dskip_ref, w_ref, bias_ref, g_ref, b_ref, out_ref):
    mod = mod_ref[0]
    x = x_ref[0]
    u = x * (1.0 + mod[1:2]) + mod[0:1]
    y = y_ref[0] + dskip_ref[...] * u
    act = jax.nn.gelu(y).astype(BF16)
    z = jnp.dot(act, w_ref[...], preferred_element_type=F32) + bias_ref[...]
    yy = z[:, :D_MODEL] * jax.nn.sigmoid(z[:, D_MODEL:])
    out_ref[0] = _layernorm(DN_ALPHA * x + (1.0 + mod[2:3]) * yy, g_ref[...], b_ref[...])


def _glu(x, y_ssm, mod, d_skip, w, bias, ln_g, ln_b, tm=256):
    bsz, s_len, _ = x.shape
    full = pl.BlockSpec((1, tm, D_MODEL), lambda b, i: (b, i, 0))
    vec = pl.BlockSpec((1, D_MODEL), lambda b, i: (0, 0))
    return pl.pallas_call(
        _glu_kernel,
        grid=(bsz, s_len // tm),
        in_specs=[full, full, pl.BlockSpec((1, 6, D_MODEL), lambda b, i: (b, 0, 0)), vec,
                  pl.BlockSpec((D_MODEL, 2 * D_MODEL), lambda b, i: (0, 0)),
                  pl.BlockSpec((1, 2 * D_MODEL), lambda b, i: (0, 0)), vec, vec],
        out_specs=full,
        out_shape=jax.ShapeDtypeStruct((bsz, s_len, D_MODEL), F32),
        compiler_params=_params(("parallel", "parallel")),
        name="glu",
    )(x, y_ssm, mod, d_skip.reshape(1, -1), w, bias.reshape(1, -1), ln_g.reshape(1, -1), ln_b.reshape(1, -1))


def _router_kernel(x_ref, mod_ref, w_ref, b_ref, cw_ref):
    mod = mod_ref[0]
    h = x_ref[0] * (1.0 + mod[4:5]) + mod[3:4]
    w = w_ref[...]
    h_hi = h.astype(BF16)
    h_lo = (h - h_hi.astype(F32)).astype(BF16)
    w_hi = w.astype(BF16)
    w_lo = (w - w_hi.astype(F32)).astype(BF16)
    logits = (jnp.dot(h_hi, w_hi, preferred_element_type=F32)
              + jnp.dot(h_hi, w_lo, preferred_element_type=F32)
              + jnp.dot(h_lo, w_hi, preferred_element_type=F32)) + b_ref[...]
    lane = lax.broadcasted_iota(jnp.int32, logits.shape, 1).astype(F32)
    logits = jnp.where(lane < N_EXPERTS, logits, NEG_INF)
    v1 = jnp.max(logits, axis=-1, keepdims=True)
    i1 = jnp.min(jnp.where(logits == v1, lane, float(LANES)), axis=-1, keepdims=True)
    rest = jnp.where(lane == i1, NEG_INF, logits)
    v2 = jnp.max(rest, axis=-1, keepdims=True)
    i2 = jnp.min(jnp.where(rest == v2, lane, float(LANES)), axis=-1, keepdims=True)
    e = jnp.exp(v2 - v1)
    g1 = 1.0 / (1.0 + e)
    g2 = e / (1.0 + e)
    cw_ref[0] = jnp.where(lane == i1, g1, 0.0) + jnp.where(lane == i2, g2, 0.0)


def _router(x, mod, rw, rb, tm=512):
    bsz, s_len, _ = x.shape
    rw_pad = jnp.zeros((D_MODEL, LANES), F32).at[:, :N_EXPERTS].set(rw)
    rb_pad = jnp.zeros((1, LANES), F32).at[0, :N_EXPERTS].set(rb)
    return pl.pallas_call(
        _router_kernel,
        grid=(bsz, s_len // tm),
        in_specs=[pl.BlockSpec((1, tm, D_MODEL), lambda b, i: (b, i, 0)),
                  pl.BlockSpec((1, 6, D_MODEL), lambda b, i: (b, 0, 0)),
                  pl.BlockSpec((D_MODEL, LANES), lambda b, i: (0, 0)),
                  pl.BlockSpec((1, LANES), lambda b, i: (0, 0))],
        out_specs=pl.BlockSpec((1, tm, LANES), lambda b, i: (b, i, 0)),
        out_shape=jax.ShapeDtypeStruct((bsz, s_len, LANES), F32),
        compiler_params=_params(("parallel", "parallel")),
        name="router",
    )(x, mod, rw_pad, rb_pad)


def _alibi_slopes():
    i = jnp.arange(N_ATTN_HEADS, dtype=F32) + 1.0
    return jnp.exp2(-8.0 * i / N_ATTN_HEADS)


def kernel(x, c, l0_ada_w, l0_ada_b, l0_w_in, l0_lam_q1, l0_lam_k1, l0_lam_q2, l0_lam_k2, l0_subln_w, l0_w_out, l0_ln1_g, l0_ln1_b, l0_ffn_w_gate, l0_ffn_w_up, l0_ffn_w_down, l0_ln2_g, l0_ln2_b, l1_ada_w, l1_ada_b, l1_a_re, l1_a_im, l1_log_dt, l1_b_re, l1_b_im, l1_c_re, l1_c_im, l1_d_skip, l1_w_glu, l1_b_glu, l1_ln1_g, l1_ln1_b, l1_router_w, l1_router_b, l1_exp_w_gate, l1_exp_w_up, l1_exp_w_down, l1_ln2_g, l1_ln2_b):
    mod0 = _ada(c, l0_ada_w, l0_ada_b)
    mod1 = _ada(c, l1_ada_w, l1_ada_b)

    q_a = jnp.full((HALF_D,), DIFF_QK_DIM ** -0.5 * LOG2E, F32)
    q_b = jnp.full((HALF_D,), HEAD_DIM ** -0.5 * LOG2E, F32)
    ones = jnp.ones((D_MODEL,), F32)
    colscale = jnp.concatenate([q_a, ones, q_b, ones]).reshape(1, -1)
    proj = _qkv(x, mod0, l0_w_in.astype(BF16), colscale)
    slopes = _alibi_slopes()
    lam_vecs = jnp.stack([l0_lam_q1, l0_lam_k1, l0_lam_q2, l0_lam_k2]).astype(F32)
    subw = jnp.tile(l0_subln_w.astype(F32), 2).reshape(1, LANES)
    oa = _diff_attention(proj, slopes[:N_HEADS_DIFF], lam_vecs, subw, 0)
    dil_slopes = tuple(2.0 ** (-8.0 * (h + 1.0) / N_ATTN_HEADS) for h in range(N_HEADS_DIFF, N_ATTN_HEADS))
    dil_outs = [_dilated_branch(proj, dil, dil_slopes) for _, dil in DIL_PAIRS]
    x = _attn_out(oa, dil_outs, x, mod0, l0_w_out.astype(BF16), l0_ln1_g, l0_ln1_b)
    x, u = _ffn(x, mod0, l0_ffn_w_gate.astype(BF16), l0_ffn_w_up.astype(BF16), l0_ffn_w_down.astype(BF16),
                l0_ln2_g, l0_ln2_b, next_mod=mod1)

    y_ssm = _s5(u, l1_a_re, l1_a_im, l1_log_dt, l1_b_re, l1_b_im, l1_c_re, l1_c_im)
    x = _glu(x, y_ssm, mod1, l1_d_skip, l1_w_glu.astype(BF16), l1_b_glu, l1_ln1_g, l1_ln1_b)
    cw = _router(x, mod1, l1_router_w, l1_router_b)
    x = _ffn(x, mod1, l1_exp_w_gate.astype(BF16), l1_exp_w_up.astype(BF16), l1_exp_w_down.astype(BF16),
             l1_ln2_g, l1_ln2_b, cw=cw)
    return x
```

```python
import functools
import math

import jax
import jax.numpy as jnp
from jax import lax
from jax.experimental import pallas as pl
from jax.experimental.pallas import tpu as pltpu

F32 = jnp.float32
BF16 = jnp.bfloat16

D_MODEL = 1024
DEPTH = 2
HEAD_DIM = 64
N_HEADS_DIFF = D_MODEL // (2 * HEAD_DIM)
DIFF_QK_DIM = HEAD_DIM // 2
N_HEADS_DIL = D_MODEL // (2 * HEAD_DIM)
N_ATTN_HEADS = N_HEADS_DIFF + N_HEADS_DIL
DIL_PAIRS = ((128, 1), (512, 4), (2048, 16))
ATTN_BLOCK = 128
SSM_GROUP_WIDTH = 16
SSM_GROUPS = D_MODEL // SSM_GROUP_WIDTH
SSM_STATE = 64
SSM_SUB = 16
D_FF = 2816
N_EXPERTS = 8
D_FF_EXPERT = 1408
DN_ALPHA = (2.0 * DEPTH) ** 0.25
EPS = 1e-5
LOG2E = 1.4426950408889634
LANES = 128
HALF_D = D_MODEL // 2
NEG_INF = float("-inf")

_NT = (((1,), (1,)), ((), ()))
_HI = lax.Precision.HIGHEST


def _params(sem, vmem_mb=48):
    return pltpu.CompilerParams(dimension_semantics=sem, vmem_limit_bytes=vmem_mb * 1024 * 1024)


def _layernorm(z, g, b):
    mu = jnp.mean(z, axis=-1, keepdims=True)
    zc = z - mu
    var = jnp.mean(zc * zc, axis=-1, keepdims=True)
    return zc * lax.rsqrt(var + EPS) * g + b


def _ada_kernel(c_ref, w_ref, b_ref, o_ref):
    h = jax.nn.silu(c_ref[...])
    o_ref[...] = jnp.dot(h.astype(BF16), w_ref[...].astype(BF16),
                         preferred_element_type=F32) + b_ref[...]


def _ada(c, w, b):
    bsz = c.shape[0]
    n = w.shape[1]
    tn = 1024
    out = pl.pallas_call(
        _ada_kernel,
        grid=(n // tn,),
        in_specs=[pl.BlockSpec((bsz, D_MODEL), lambda j: (0, 0)),
                  pl.BlockSpec((D_MODEL, tn), lambda j: (0, j)),
                  pl.BlockSpec((1, tn), lambda j: (0, j))],
        out_specs=pl.BlockSpec((bsz, tn), lambda j: (0, j)),
        out_shape=jax.ShapeDtypeStruct((bsz, n), F32),
        compiler_params=_params(("arbitrary",)),
        name="ada",
    )(c, w, b.reshape(1, n))
    return out.reshape(bsz, 6, D_MODEL)


def _qkv_kernel(x_ref, mod_ref, w_ref, cs_ref, o_ref, *rest, dils):
    perm_refs, stage_ref = rest[:-1], rest[-1]
    tm = x_ref.shape[1]
    mod = mod_ref[0]
    h = (x_ref[0] * (1.0 + mod[1:2]) + mod[0:1]).astype(BF16)
    for j in range(6):
        sl = slice(j * HALF_D, (j + 1) * HALF_D)
        acc = jnp.dot(h, w_ref[:, sl], preferred_element_type=F32) * cs_ref[:, sl]
        o_ref[0, :, sl] = acc.astype(BF16)
        if j >= 3 and dils:
            for c in range(HALF_D // LANES):
                stage_ref[(j - 3) * (HALF_D // LANES) + c] = acc[:, c * LANES:(c + 1) * LANES]
    for dil, p_ref in zip(dils, perm_refs):
        for r in range(dil):
            for c in range(stage_ref.shape[0]):
                rows = stage_ref[c, pl.ds(r, tm // dil, stride=dil), :]
                p_ref[0, r, :, c * LANES:(c + 1) * LANES] = rows.astype(BF16)


def _qkv(x, mod, w, colscale, dils, tm=512):
    bsz, s_len, _ = x.shape
    n = w.shape[1]
    nb = n // 2
    out_specs = [pl.BlockSpec((1, tm, n), lambda b, i: (b, i, 0))]
    out_shape = [jax.ShapeDtypeStruct((bsz, s_len, n), BF16)]
    for dil in dils:
        out_specs.append(pl.BlockSpec((1, dil, tm // dil, nb), lambda b, i: (b, 0, i, 0)))
        out_shape.append(jax.ShapeDtypeStruct((bsz, dil, s_len // dil, nb), BF16))
    return pl.pallas_call(
        functools.partial(_qkv_kernel, dils=dils),
        grid=(bsz, s_len // tm),
        in_specs=[pl.BlockSpec((1, tm, D_MODEL), lambda b, i: (b, i, 0)),
                  pl.BlockSpec((1, 6, D_MODEL), lambda b, i: (b, 0, 0)),
                  pl.BlockSpec((D_MODEL, n), lambda b, i: (0, 0)),
                  pl.BlockSpec((1, n), lambda b, i: (0, 0))],
        out_specs=out_specs,
        out_shape=out_shape,
        scratch_shapes=[pltpu.VMEM((nb // LANES, tm, LANES), F32)],
        compiler_params=_params(("parallel", "parallel")),
        name="qkv",
    )(x, mod, w, colscale)


def _diff_kernel(slope_ref, lam_ref, subw_ref, q_ref, k_ref, v_ref, o_ref,
                 acc_ref, m_ref, mask_ref, vat_ref, sa_ref, sb_ref, *, tk, lam_init):
    tq = 2 * tk
    pr = pl.program_id(1)
    qi = pl.program_id(2)
    n_kb = k_ref.shape[1] // tk
    sl2 = [slope_ref[2 * pr + hh] * LOG2E for hh in range(2)]

    @pl.when(qi == 0)
    def _():
        kr = lax.broadcasted_iota(jnp.int32, (tk, tq), 0)
        qc = lax.broadcasted_iota(jnp.int32, (tk, tq), 1)
        mask_ref[0] = jnp.where(qc >= kr, 0.0, NEG_INF)
        mask_ref[1] = jnp.where(qc - tk >= kr, 0.0, NEG_INF)
        row = lax.broadcasted_iota(jnp.int32, (LANES, tk), 0)
        for j in range(n_kb):
            vt = v_ref[0, j * tk:(j + 1) * tk, :].astype(F32).T
            vat_ref[0, j] = jnp.where(row < HEAD_DIM, vt, 1.0).astype(BF16)
            vat_ref[1, j] = jnp.where(row >= HEAD_DIM, vt, 1.0).astype(BF16)

    lane = lax.broadcasted_iota(jnp.int32, (1, LANES), 1)
    q = q_ref[0]
    feats = []
    for hh in range(2):
        slv = jnp.zeros((1, LANES), F32) + sl2[hh]
        hi = slv.astype(BF16).astype(F32)
        r1 = slv - hi
        lo = r1.astype(BF16).astype(F32)
        lo2 = (r1 - lo).astype(BF16).astype(F32)
        feat = jnp.where(lane == 0, hi, jnp.where(lane == 1, lo, jnp.where(lane == 2, lo2, 0.0)))
        feats.append(jnp.broadcast_to(feat, (tq, LANES)).astype(BF16))
    q_aug = [jnp.concatenate([jnp.where(lane // DIFF_QK_DIM == m, q, jnp.zeros_like(q)), feats[m // 2]], axis=1)
             for m in range(4)]
    key_pos = lax.broadcasted_iota(jnp.int32, (tk, LANES), 0).astype(F32)
    pos_feat = jnp.where(lane < 3, key_pos, 0.0).astype(BF16)
    acc_ref[...] = jnp.zeros_like(acc_ref)
    m_ref[...] = jnp.full_like(m_ref, NEG_INF)

    def scores(j, s_ref):
        off = pl.multiple_of(j * tk, tk)
        k_aug = jnp.concatenate([k_ref[0, pl.ds(off, tk), :], pos_feat], axis=1)
        for m in range(4):
            s_ref[m] = lax.dot_general(k_aug, q_aug[m], _NT, preferred_element_type=F32)

    def softmax_pv(j, s_ref, variant):
        blk_off = (qi * tq - j * tk).astype(F32)
        ps, alphas = [], []
        for m in range(4):
            t = s_ref[m] if variant == 0 else s_ref[m] + mask_ref[variant - 1]
            c = -sl2[m // 2] * blk_off
            m_old = m_ref[m]
            m_new = jnp.maximum(m_old, jnp.max(t, axis=0, keepdims=True) + c)
            alphas.append(jnp.exp2(m_old - m_new))
            ps.append(jnp.exp2(t - (m_new - c)).astype(BF16))
            m_ref[m] = m_new
        for m in range(4):
            acc_ref[m] = alphas[m] * acc_ref[m] + jnp.dot(vat_ref[m // 2, j], ps[m], preferred_element_type=F32)

    scores(0, sa_ref)

    def body(i, carry):
        j = 2 * i
        scores(j + 1, sb_ref)
        softmax_pv(j, sa_ref, 0)
        scores(j + 2, sa_ref)
        softmax_pv(j + 1, sb_ref, 0)
        return carry

    lax.fori_loop(0, qi, body, 0)
    scores(2 * qi + 1, sb_ref)
    softmax_pv(2 * qi, sa_ref, 1)
    softmax_pv(2 * qi + 1, sb_ref, 2)

    lam_v = lam_ref[...]
    lam = (jnp.exp(jnp.sum(lam_v[0:1] * lam_v[1:2], axis=-1, keepdims=True))
           - jnp.exp(jnp.sum(lam_v[2:3] * lam_v[3:4], axis=-1, keepdims=True)) + lam_init)
    normed = []
    for hh in range(2):
        o_rows = slice(hh * HEAD_DIM, (hh + 1) * HEAD_DIM)
        l_row = slice((1 - hh) * HEAD_DIM, (1 - hh) * HEAD_DIM + 1)
        outs = []
        for mm in range(2):
            a = acc_ref[2 * hh + mm]
            outs.append(a[o_rows] / a[l_row])
        d = outs[0] - lam * outs[1]
        ms = jnp.mean(d * d, axis=0, keepdims=True)
        normed.append(d * lax.rsqrt(ms + EPS))
    y_t = jnp.concatenate(normed, axis=0) * subw_ref[...] * (1.0 - lam_init)
    o_ref[0] = y_t.T.astype(BF16)


def _diff_attention(proj, slopes, lam_vecs, subw, layer_idx, tk=256):
    bsz, s_len, _ = proj.shape
    tq = 2 * tk
    n_pairs = N_HEADS_DIFF // 2
    lam_init = 0.8 - 0.6 * math.exp(-0.3 * layer_idx)
    kern = functools.partial(_diff_kernel, tk=tk, lam_init=lam_init)
    return pl.pallas_call(
        kern,
        grid=(bsz, n_pairs, s_len // tq),
        in_specs=[pl.BlockSpec(memory_space=pltpu.SMEM),
                  pl.BlockSpec((4, DIFF_QK_DIM), lambda b, p, i: (0, 0)),
                  pl.BlockSpec((LANES, 1), lambda b, p, i: (0, 0)),
                  pl.BlockSpec((1, tq, LANES), lambda b, p, i: (b, i, p)),
                  pl.BlockSpec((1, s_len, LANES), lambda b, p, i: (b, 0, n_pairs + p)),
                  pl.BlockSpec((1, s_len, LANES), lambda b, p, i: (b, 0, 2 * n_pairs + p))],
        out_specs=pl.BlockSpec((1, tq, LANES), lambda b, p, i: (b, i, p)),
        out_shape=jax.ShapeDtypeStruct((bsz, s_len, HALF_D), BF16),
        scratch_shapes=[pltpu.VMEM((4, LANES, tq), F32),
                        pltpu.VMEM((4, 1, tq), F32),
                        pltpu.VMEM((2, tk, tq), F32),
                        pltpu.VMEM((2, s_len // tk, LANES, tk), BF16),
                        pltpu.VMEM((4, tk, tq), F32),
                        pltpu.VMEM((4, tk, tq), F32)],
        compiler_params=_params(("parallel", "parallel", "arbitrary")),
        name="diff_attn",
    )(slopes, lam_vecs, subw, proj, proj, proj)


def _dil_kernel(q_ref, kp_ref, kc_ref, vp_ref, vc_ref, o_ref, l_ref, bias_ref, *, dil, slopes):
    blk = ATTN_BLOCK
    n = pl.program_id(2)
    first = (pl.program_id(0) == 0) & (pl.program_id(1) == 0) & (n == 0)

    @pl.when(first)
    def _():
        qi = lax.broadcasted_iota(jnp.int32, (blk, 2 * blk), 0)
        kj = lax.broadcasted_iota(jnp.int32, (blk, 2 * blk), 1)
        dist = qi + blk - kj
        valid = (dist >= 0) & (dist <= blk)
        dist_f = (dist * dil).astype(F32)
        for h in range(N_HEADS_DIL):
            bias = (-slopes[h] * LOG2E) * dist_f
            bias_ref[h, 0] = jnp.where(valid, bias, NEG_INF)
            bias_ref[h, 1] = jnp.where(valid & (kj >= blk), bias, NEG_INF)

    sel = (n == 0).astype(jnp.int32)
    lane = lax.broadcasted_iota(jnp.int32, (1, LANES), 1)
    even = lane < HEAD_DIM
    one = jnp.ones((), BF16)
    for p in range(N_HEADS_DIL // 2):
        sl = slice(p * LANES, (p + 1) * LANES)
        qp = q_ref[:, sl]
        kcat = jnp.concatenate([kp_ref[:, sl], kc_ref[:, sl]], axis=0)
        vcat = jnp.concatenate([vp_ref[:, sl], vc_ref[:, sl]], axis=0)
        accs, ms = [], []
        for hh in range(2):
            keep = even if hh == 0 else jnp.logical_not(even)
            qh = jnp.where(keep, qp, jnp.zeros_like(qp))
            va = jnp.where(keep, vcat, one)
            s = lax.dot_general(qh, kcat, _NT, preferred_element_type=F32)
            t = s + bias_ref[2 * p + hh, sel]
            m = jnp.max(t, axis=-1, keepdims=True)
            pm = jnp.exp2(t - m)
            accs.append(jnp.dot(pm.astype(BF16), va, preferred_element_type=F32))
            ms.append(m)
        o_un = jnp.where(even, accs[0], accs[1])
        l_sum = pltpu.roll(jnp.where(even, accs[1], accs[0]), HEAD_DIM, 1)
        m_pair = jnp.where(even, ms[0], ms[1])
        o_ref[:, sl] = o_un / l_sum
        l_ref[:, sl] = m_pair + jnp.log2(l_sum)


def _dilated_branch(qkv_planes, dil, slopes, col0):
    bsz, _, sub_len, _ = qkv_planes.shape
    blk = ATTN_BLOCK
    nb = sub_len // blk
    kern = functools.partial(_dil_kernel, dil=dil, slopes=slopes)

    def cur(col):
        return pl.BlockSpec((None, None, blk, HALF_D), lambda b, r, n: (b, r, n, col0 + col))

    def prev(col):
        return pl.BlockSpec((None, None, blk, HALF_D), lambda b, r, n: (b, r, jnp.maximum(n - 1, 0), col0 + col))

    out_spec = pl.BlockSpec((None, None, blk, HALF_D), lambda b, r, n: (b, r, n, 0))
    return pl.pallas_call(
        kern,
        grid=(bsz, dil, nb),
        in_specs=[cur(0), prev(1), cur(1), prev(2), cur(2)],
        out_specs=[out_spec, out_spec],
        out_shape=[jax.ShapeDtypeStruct((bsz, dil, sub_len, HALF_D), F32)] * 2,
        scratch_shapes=[pltpu.VMEM((N_HEADS_DIL, 2, blk, 2 * blk), F32)],
        compiler_params=_params(("arbitrary", "arbitrary", "arbitrary")),
        name=f"dilated_d{dil}",
    )(*([qkv_planes] * 5))


def _attn_out_kernel(oa_ref, o1_ref, l1_ref, o2_ref, l2_ref, o3_ref, l3_ref, x_ref, mod_ref,
                     w_ref, g_ref, b_ref, out_ref, tok_ref, ob_ref, *, dils):
    tm = x_ref.shape[1]
    branch_refs = ((o1_ref, l1_ref), (o2_ref, l2_ref), (o3_ref, l3_ref))
    for c in range(HALF_D // LANES):
        cols = slice(c * LANES, (c + 1) * LANES)
        vals = []
        for g, (dil, pair) in enumerate(zip(dils, branch_refs)):
            for a, src_ref in enumerate(pair):
                if dil == 1:
                    vals.append(src_ref[0, 0, :, cols])
                else:
                    for r in range(dil):
                        tok_ref[g, a, pl.ds(r, tm // dil, stride=dil), :] = src_ref[0, r, :, cols]
                    vals.append(tok_ref[g, a])
        o1, l1, o2, l2, o3, l3 = vals
        mx = jnp.maximum(jnp.maximum(l1, l2), l3)
        e1, e2, e3 = jnp.exp2(l1 - mx), jnp.exp2(l2 - mx), jnp.exp2(l3 - mx)
        ob_ref[:, cols] = ((e1 * o1 + e2 * o2 + e3 * o3) / (e1 + e2 + e3)).astype(BF16)
    y = (jnp.dot(oa_ref[0], w_ref[:HALF_D], preferred_element_type=F32)
         + jnp.dot(ob_ref[...], w_ref[HALF_D:], preferred_element_type=F32))
    mod = mod_ref[0]
    z = DN_ALPHA * x_ref[0] + (1.0 + mod[2:3]) * y
    out_ref[0] = _layernorm(z, g_ref[...], b_ref[...])


def _attn_out(oa, dil_outs, dils, x, mod, w, ln_g, ln_b, tm=256):
    bsz, s_len, _ = x.shape
    half = pl.BlockSpec((1, tm, HALF_D), lambda b, i: (b, i, 0))
    full = pl.BlockSpec((1, tm, D_MODEL), lambda b, i: (b, i, 0))
    vec = pl.BlockSpec((1, D_MODEL), lambda b, i: (0, 0))
    plane_specs = []
    for dil in dils:
        spec = pl.BlockSpec((1, dil, tm // dil, HALF_D), lambda b, i: (b, 0, i, 0))
        plane_specs += [spec, spec]
    flat = [a for pair in dil_outs for a in pair]
    return pl.pallas_call(
        functools.partial(_attn_out_kernel, dils=dils),
        grid=(bsz, s_len // tm),
        in_specs=[half] + plane_specs + [full, pl.BlockSpec((1, 6, D_MODEL), lambda b, i: (b, 0, 0)),
                                         pl.BlockSpec((D_MODEL, D_MODEL), lambda b, i: (0, 0)), vec, vec],
        out_specs=full,
        out_shape=jax.ShapeDtypeStruct((bsz, s_len, D_MODEL), F32),
        scratch_shapes=[pltpu.VMEM((len(dils), 2, tm, LANES), F32), pltpu.VMEM((tm, HALF_D), BF16)],
        compiler_params=_params(("parallel", "parallel")),
        name="attn_out",
    )(oa, *flat, x, mod, w, ln_g.reshape(1, -1), ln_b.reshape(1, -1))


def _ffn_kernel(*refs, gated, emit_next):
    it = iter(refs)
    x_ref, mod_ref = next(it), next(it)
    cw_ref = next(it) if gated else None
    wg_ref, wu_ref, wd_ref, g_ref, b_ref = next(it), next(it), next(it), next(it), next(it)
    nmod_ref = next(it) if emit_next else None
    out_ref = next(it)
    nxt_ref = next(it) if emit_next else None
    h_ref, acc_ref = next(it), next(it)

    j = pl.program_id(2)
    mod = mod_ref[0]

    @pl.when(j == 0)
    def _():
        h_ref[...] = (x_ref[0] * (1.0 + mod[4:5]) + mod[3:4]).astype(BF16)
        acc_ref[...] = jnp.zeros_like(acc_ref)

    h = h_ref[...]
    wg = wg_ref[0] if gated else wg_ref[...]
    wu = wu_ref[0] if gated else wu_ref[...]
    wd = wd_ref[0] if gated else wd_ref[...]
    gate = jnp.dot(h, wg, preferred_element_type=F32)
    up = jnp.dot(h, wu, preferred_element_type=F32)
    act = (jax.nn.silu(gate) * up).astype(BF16)
    y = jnp.dot(act, wd, preferred_element_type=F32)
    if gated:
        lane = lax.broadcasted_iota(jnp.int32, (1, LANES), 1)
        cw = jnp.sum(jnp.where(lane == j, cw_ref[0], 0.0), axis=-1, keepdims=True)
        y = cw * y
    acc_ref[...] += y

    @pl.when(j == pl.num_programs(2) - 1)
    def _():
        z = DN_ALPHA * x_ref[0] + (1.0 + mod[5:6]) * acc_ref[...]
        out = _layernorm(z, g_ref[...], b_ref[...])
        out_ref[0] = out
        if emit_next:
            nmod = nmod_ref[0]
            nxt_ref[0] = (out * (1.0 + nmod[1:2]) + nmod[0:1]).astype(BF16)


def _ffn(x, mod, wg, wu, wd, ln_g, ln_b, *, cw=None, next_mod=None, tm=512, tf=1408):
    bsz, s_len, _ = x.shape
    gated = cw is not None
    emit_next = next_mod is not None
    full = pl.BlockSpec((1, tm, D_MODEL), lambda b, i, j: (b, i, 0))
    modspec = pl.BlockSpec((1, 6, D_MODEL), lambda b, i, j: (b, 0, 0))
    vec = pl.BlockSpec((1, D_MODEL), lambda b, i, j: (0, 0))
    if gated:
        n_inner = wg.shape[0]
        w_in_spec = pl.BlockSpec((1, D_MODEL, wg.shape[2]), lambda b, i, j: (j, 0, 0))
        w_dn_spec = pl.BlockSpec((1, wd.shape[1], D_MODEL), lambda b, i, j: (j, 0, 0))
    else:
        n_inner = wg.shape[1] // tf
        w_in_spec = pl.BlockSpec((D_MODEL, tf), lambda b, i, j: (0, j))
        w_dn_spec = pl.BlockSpec((tf, D_MODEL), lambda b, i, j: (j, 0))
    in_specs = [full, modspec]
    args = [x, mod]
    if gated:
        in_specs.append(pl.BlockSpec((1, tm, LANES), lambda b, i, j: (b, i, 0)))
        args.append(cw)
    in_specs += [w_in_spec, w_in_spec, w_dn_spec, vec, vec]
    args += [wg, wu, wd, ln_g.reshape(1, -1), ln_b.reshape(1, -1)]
    out_specs = [full]
    out_shape = [jax.ShapeDtypeStruct((bsz, s_len, D_MODEL), F32)]
    if emit_next:
        in_specs.append(modspec)
        args.append(next_mod)
        out_specs.append(full)
        out_shape.append(jax.ShapeDtypeStruct((bsz, s_len, D_MODEL), BF16))
    res = pl.pallas_call(
        functools.partial(_ffn_kernel, gated=gated, emit_next=emit_next),
        grid=(bsz, s_len // tm, n_inner),
        in_specs=in_specs,
        out_specs=out_specs,
        out_shape=out_shape,
        scratch_shapes=[pltpu.VMEM((tm, D_MODEL), BF16), pltpu.VMEM((tm, D_MODEL), F32)],
        compiler_params=_params(("parallel", "parallel", "arbitrary")),
        name="moe_ffn" if gated else "ffn",
    )(*args)
    return res if emit_next else res[0]


def _s5_kernel(ar_r_ref, ai_r_ref, ar_c_ref, ai_c_ref, ldt_ref, brep_re_ref, brep_im_ref,
               bt_re_ref, bt_im_ref, c_re_ref, c_im_ref, u_ref, y_ref,
               sx_re, sx_im, h_re, h_im):
    sub = SSM_SUB
    gw = SSM_GROUP_WIDTH
    n_lane = sub * gw
    dt = jnp.exp(ldt_ref[0])

    def discretise(ar, ai):
        mag = jnp.exp(dt * ar)
        abr, abi = mag * jnp.cos(dt * ai), mag * jnp.sin(dt * ai)
        den = ar * ar + ai * ai
        nr, ni = abr - 1.0, abi
        return abr, abi, (nr * ar + ni * ai) / den, (ni * ar - nr * ai) / den

    abr, abi, zr, zi = discretise(ar_r_ref[0], ai_r_ref[0])
    _, _, zrc, zic = discretise(ar_c_ref[0], ai_c_ref[0])
    bt_r, bt_i = bt_re_ref[0], bt_im_ref[0]
    bbt_r, bbt_i = zr * bt_r - zi * bt_i, zr * bt_i + zi * bt_r
    brep_r, brep_i = brep_re_ref[0], brep_im_ref[0]
    bbrep_r, bbrep_i = zrc * brep_r - zic * brep_i, zrc * brep_i + zic * brep_r

    pw = []
    pr, pi = jnp.ones_like(abr), jnp.zeros_like(abr)
    for _ in range(sub + 1):
        pw.append((pr, pi))
        pr, pi = pr * abr - pi * abi, pr * abi + pi * abr
    cr, ci = c_re_ref[0], c_im_ref[0]
    m_re = [cr * a - ci * b for a, b in pw]
    m_im = [cr * b + ci * a for a, b in pw]

    kk = (jnp.dot(jnp.concatenate(m_re[:sub], axis=0), bbrep_r, precision=_HI, preferred_element_type=F32)
          - jnp.dot(jnp.concatenate(m_im[:sub], axis=0), bbrep_i, precision=_HI, preferred_element_type=F32))
    lane_slot = lax.broadcasted_iota(jnp.int32, (n_lane, n_lane), 1) // gw
    tt = jnp.zeros((n_lane, n_lane), F32)
    for s in range(sub):
        if s == 0:
            shifted = kk
        else:
            shifted = jnp.concatenate([jnp.zeros((s * gw, n_lane), F32), kk[:n_lane - s * gw]], axis=0)
        tt = jnp.where(lane_slot == s, shifted, tt)

    w_st_r = jnp.concatenate([bbt_r * pw[sub - 1 - s][0] - bbt_i * pw[sub - 1 - s][1] for s in range(sub)], axis=0)
    w_st_i = jnp.concatenate([bbt_r * pw[sub - 1 - s][1] + bbt_i * pw[sub - 1 - s][0] for s in range(sub)], axis=0)
    w_out_r = jnp.concatenate(m_re[1:], axis=0)
    w_out_i = jnp.concatenate(m_im[1:], axis=0)

    u = u_ref[0]
    sx_re[...] = jnp.dot(u, w_st_r.astype(BF16), preferred_element_type=F32)
    sx_im[...] = jnp.dot(u, w_st_i.astype(BF16), preferred_element_type=F32)

    n_batch = 8
    alr = jnp.broadcast_to(pw[sub][0], (n_batch, SSM_STATE))
    ali = jnp.broadcast_to(pw[sub][1], (n_batch, SSM_STATE))

    def body(j, carry):
        hr, hi = carry
        off = pl.multiple_of(j * n_batch, n_batch)
        h_re[pl.ds(off, n_batch), :] = hr
        h_im[pl.ds(off, n_batch), :] = hi
        sr = sx_re[pl.ds(off, n_batch), :]
        si = sx_im[pl.ds(off, n_batch), :]
        return alr * hr - ali * hi + sr, alr * hi + ali * hr + si

    zero = jnp.zeros((n_batch, SSM_STATE), F32)
    lax.fori_loop(0, u.shape[0] // n_batch, body, (zero, zero))

    y = lax.dot_general(u, tt.astype(BF16), _NT, preferred_element_type=F32)
    y += lax.dot_general(h_re[...].astype(BF16), w_out_r.astype(BF16), _NT, preferred_element_type=F32)
    y -= lax.dot_general(h_im[...].astype(BF16), w_out_i.astype(BF16), _NT, preferred_element_type=F32)
    y_ref[0] = y


def _s5(u, a_re, a_im, log_dt, b_re, b_im, c_re, c_im):
    bsz, s_len, _ = u.shape
    assert bsz == 8, "rows of the SSM layout hold one batch per sublane"
    g, p, gw, sub = SSM_GROUPS, SSM_STATE, SSM_GROUP_WIDTH, SSM_SUB
    nj = s_len // sub
    rows = nj * bsz
    ur = u.reshape(bsz, nj, sub, g, gw).transpose(3, 1, 0, 2, 4).reshape(g, rows, sub * gw)
    grp = lambda *shape: pl.BlockSpec((1,) + shape, lambda i: (i,) + (0,) * len(shape))
    yr = pl.pallas_call(
        _s5_kernel,
        grid=(g,),
        in_specs=[grp(1, p), grp(1, p), grp(p, 1), grp(p, 1), grp(1, 1),
                  grp(p, sub * gw), grp(p, sub * gw), grp(gw, p), grp(gw, p), grp(gw, p), grp(gw, p),
                  grp(rows, sub * gw)],
        out_specs=grp(rows, sub * gw),
        out_shape=jax.ShapeDtypeStruct((g, rows, sub * gw), F32),
        scratch_shapes=[pltpu.VMEM((rows, p), F32)] * 4,
        compiler_params=_params(("parallel",)),
        name="s5",
    )(a_re.reshape(g, 1, p), a_im.reshape(g, 1, p), a_re.reshape(g, p, 1), a_im.reshape(g, p, 1),
      log_dt.reshape(g, 1, 1), jnp.tile(b_re, (1, 1, sub)), jnp.tile(b_im, (1, 1, sub)),
      jnp.swapaxes(b_re, 1, 2), jnp.swapaxes(b_im, 1, 2), c_re, c_im, ur)
    return yr.reshape(g, nj, bsz, sub, gw).transpose(2, 1, 3, 0, 4).reshape(bsz, s_len, D_MODEL)


def _glu_kernel(x_ref, y_ref, mod_ref, dskip_ref, w_ref, bias_ref, g_ref, b_ref, out_ref):
    mod = mod_ref[0]
    x = x_ref[0]
    u = x * (1.0 + mod[1:2]) + mod[0:1]
    y = y_ref[0] + dskip_ref[...] * u
    act = jax.nn.gelu(y).astype(BF16)
    z = jnp.dot(act, w_ref[...], preferred_element_type=F32) + bias_ref[...]
    yy = z[:, :D_MODEL] * jax.nn.sigmoid(z[:, D_MODEL:])
    out_ref[0] = _layernorm(DN_ALPHA * x + (1.0 + mod[2:3]) * yy, g_ref[...], b_ref[...])


def _glu(x, y_ssm, mod, d_skip, w, bias, ln_g, ln_b, tm=256):
    bsz, s_len, _ = x.shape
    full = pl.BlockSpec((1, tm, D_MODEL), lambda b, i: (b, i, 0))
    vec = pl.BlockSpec((1, D_MODEL), lambda b, i: (0, 0))
    return pl.pallas_call(
        _glu_kernel,
        grid=(bsz, s_len // tm),
        in_specs=[full, full, pl.BlockSpec((1, 6, D_MODEL), lambda b, i: (b, 0, 0)), vec,
                  pl.BlockSpec((D_MODEL, 2 * D_MODEL), lambda b, i: (0, 0)),
                  pl.BlockSpec((1, 2 * D_MODEL), lambda b, i: (0, 0)), vec, vec],
        out_specs=full,
        out_shape=jax.ShapeDtypeStruct((bsz, s_len, D_MODEL), F32),
        compiler_params=_params(("parallel", "parallel")),
        name="glu",
    )(x, y_ssm, mod, d_skip.reshape(1, -1), w, bias.reshape(1, -1), ln_g.reshape(1, -1), ln_b.reshape(1, -1))


def _router_kernel(x_ref, mod_ref, w_ref, b_ref, cw_ref):
    mod = mod_ref[0]
    h = x_ref[0] * (1.0 + mod[4:5]) + mod[3:4]
    w = w_ref[...]
    h_hi = h.astype(BF16)
    h_lo = (h - h_hi.astype(F32)).astype(BF16)
    w_hi = w.astype(BF16)
    w_lo = (w - w_hi.astype(F32)).astype(BF16)
    logits = (jnp.dot(h_hi, w_hi, preferred_element_type=F32)
              + jnp.dot(h_hi, w_lo, preferred_element_type=F32)
              + jnp.dot(h_lo, w_hi, preferred_element_type=F32)) + b_ref[...]
    lane = lax.broadcasted_iota(jnp.int32, logits.shape, 1).astype(F32)
    logits = jnp.where(lane < N_EXPERTS, logits, NEG_INF)
    v1 = jnp.max(logits, axis=-1, keepdims=True)
    i1 = jnp.min(jnp.where(logits == v1, lane, float(LANES)), axis=-1, keepdims=True)
    rest = jnp.where(lane == i1, NEG_INF, logits)
    v2 = jnp.max(rest, axis=-1, keepdims=True)
    i2 = jnp.min(jnp.where(rest == v2, lane, float(LANES)), axis=-1, keepdims=True)
    e = jnp.exp(v2 - v1)
    g1 = 1.0 / (1.0 + e)
    g2 = e / (1.0 + e)
    cw_ref[0] = jnp.where(lane == i1, g1, 0.0) + jnp.where(lane == i2, g2, 0.0)


def _router(x, mod, rw, rb, tm=512):
    bsz, s_len, _ = x.shape
    rw_pad = jnp.zeros((D_MODEL, LANES), F32).at[:, :N_EXPERTS].set(rw)
    rb_pad = jnp.zeros((1, LANES), F32).at[0, :N_EXPERTS].set(rb)
    return pl.pallas_call(
        _router_kernel,
        grid=(bsz, s_len // tm),
        in_specs=[pl.BlockSpec((1, tm, D_MODEL), lambda b, i: (b, i, 0)),
                  pl.BlockSpec((1, 6, D_MODEL), lambda b, i: (b, 0, 0)),
                  pl.BlockSpec((D_MODEL, LANES), lambda b, i: (0, 0)),
                  pl.BlockSpec((1, LANES), lambda b, i: (0, 0))],
        out_specs=pl.BlockSpec((1, tm, LANES), lambda b, i: (b, i, 0)),
        out_shape=jax.ShapeDtypeStruct((bsz, s_len, LANES), F32),
        compiler_params=_params(("parallel", "parallel")),
        name="router",
    )(x, mod, rw_pad, rb_pad)


def _alibi_slopes():
    i = jnp.arange(N_ATTN_HEADS, dtype=F32) + 1.0
    return jnp.exp2(-8.0 * i / N_ATTN_HEADS)


def kernel(x, c, l0_ada_w, l0_ada_b, l0_w_in, l0_lam_q1, l0_lam_k1, l0_lam_q2, l0_lam_k2, l0_subln_w, l0_w_out, l0_ln1_g, l0_ln1_b, l0_ffn_w_gate, l0_ffn_w_up, l0_ffn_w_down, l0_ln2_g, l0_ln2_b, l1_ada_w, l1_ada_b, l1_a_re, l1_a_im, l1_log_dt, l1_b_re, l1_b_im, l1_c_re, l1_c_im, l1_d_skip, l1_w_glu, l1_b_glu, l1_ln1_g, l1_ln1_b, l1_router_w, l1_router_b, l1_exp_w_gate, l1_exp_w_up, l1_exp_w_down, l1_ln2_g, l1_ln2_b):
    mod0 = _ada(c, l0_ada_w, l0_ada_b)
    mod1 = _ada(c, l1_ada_w, l1_ada_b)

    q_a = jnp.full((HALF_D,), DIFF_QK_DIM ** -0.5 * LOG2E, F32)
    q_b = jnp.full((HALF_D,), HEAD_DIM ** -0.5 * LOG2E, F32)
    ones = jnp.ones((D_MODEL,), F32)
    colscale = jnp.concatenate([q_a, ones, q_b, ones]).reshape(1, -1)
    dils = tuple(dil for _, dil in DIL_PAIRS)
    proj, *planes = _qkv(x, mod0, l0_w_in.astype(BF16), colscale, dils[1:])
    slopes = _alibi_slopes()
    lam_vecs = jnp.stack([l0_lam_q1, l0_lam_k1, l0_lam_q2, l0_lam_k2]).astype(F32)
    subw = jnp.tile(l0_subln_w.astype(F32), 2).reshape(LANES, 1)
    oa = _diff_attention(proj, slopes[:N_HEADS_DIFF], lam_vecs, subw, 0)
    dil_slopes = tuple(2.0 ** (-8.0 * (h + 1.0) / N_ATTN_HEADS) for h in range(N_HEADS_DIFF, N_ATTN_HEADS))
    assert dils[0] == 1
    dil_outs = [_dilated_branch(proj[:, None], 1, dil_slopes, 3)]
    dil_outs += [_dilated_branch(pln, dil, dil_slopes, 0) for pln, dil in zip(planes, dils[1:])]
    x = _attn_out(oa, dil_outs, dils, x, mod0, l0_w_out.astype(BF16), l0_ln1_g, l0_ln1_b)
    x, u = _ffn(x, mod0, l0_ffn_w_gate.astype(BF16), l0_ffn_w_up.astype(BF16), l0_ffn_w_down.astype(BF16),
                l0_ln2_g, l0_ln2_b, next_mod=mod1)

    y_ssm = _s5(u, l1_a_re, l1_a_im, l1_log_dt, l1_b_re, l1_b_im, l1_c_re, l1_c_im)
    x = _glu(x, y_ssm, mod1, l1_d_skip, l1_w_glu.astype(BF16), l1_b_glu, l1_ln1_g, l1_ln1_b)
    cw = _router(x, mod1, l1_router_w, l1_router_b)
    x = _ffn(x, mod1, l1_exp_w_gate.astype(BF16), l1_exp_w_up.astype(BF16), l1_exp_w_down.astype(BF16),
             l1_ln2_g, l1_ln2_b, cw=cw)
    return x
```

```python
import functools
import math

import jax
import jax.numpy as jnp
from jax import lax
from jax.experimental import pallas as pl
from jax.experimental.pallas import tpu as pltpu

F32 = jnp.float32
BF16 = jnp.bfloat16

D_MODEL = 1024
DEPTH = 2
HEAD_DIM = 64
N_HEADS_DIFF = D_MODEL // (2 * HEAD_DIM)
DIFF_QK_DIM = HEAD_DIM // 2
N_HEADS_DIL = D_MODEL // (2 * HEAD_DIM)
N_ATTN_HEADS = N_HEADS_DIFF + N_HEADS_DIL
DIL_PAIRS = ((128, 1), (512, 4), (2048, 16))
ATTN_BLOCK = 128
SSM_GROUP_WIDTH = 16
SSM_GROUPS = D_MODEL // SSM_GROUP_WIDTH
SSM_STATE = 64
SSM_SUB = 16
D_FF = 2816
N_EXPERTS = 8
D_FF_EXPERT = 1408
DN_ALPHA = (2.0 * DEPTH) ** 0.25
EPS = 1e-5
LOG2E = 1.4426950408889634
LANES = 128
HALF_D = D_MODEL // 2
NEG_INF = float("-inf")

_NT = (((1,), (1,)), ((), ()))
_HI = lax.Precision.HIGHEST


def _params(sem, vmem_mb=48):
    return pltpu.CompilerParams(dimension_semantics=sem, vmem_limit_bytes=vmem_mb * 1024 * 1024)


def _layernorm(z, g, b):
    mu = jnp.mean(z, axis=-1, keepdims=True)
    zc = z - mu
    var = jnp.mean(zc * zc, axis=-1, keepdims=True)
    return zc * lax.rsqrt(var + EPS) * g + b


def _ada_kernel(c_ref, w_ref, b_ref, o_ref):
    h = jax.nn.silu(c_ref[...])
    o_ref[...] = jnp.dot(h.astype(BF16), w_ref[...].astype(BF16),
                         preferred_element_type=F32) + b_ref[...]


def _ada(c, w, b):
    bsz = c.shape[0]
    n = w.shape[1]
    tn = 1024
    out = pl.pallas_call(
        _ada_kernel,
        grid=(n // tn,),
        in_specs=[pl.BlockSpec((bsz, D_MODEL), lambda j: (0, 0)),
                  pl.BlockSpec((D_MODEL, tn), lambda j: (0, j)),
                  pl.BlockSpec((1, tn), lambda j: (0, j))],
        out_specs=pl.BlockSpec((bsz, tn), lambda j: (0, j)),
        out_shape=jax.ShapeDtypeStruct((bsz, n), F32),
        compiler_params=_params(("arbitrary",)),
        name="ada",
    )(c, w, b.reshape(1, n))
    return out.reshape(bsz, 6, D_MODEL)


def _qkv_kernel(x_ref, mod_ref, w_ref, cs_ref, o_ref, *rest, dils):
    perm_refs, stage_ref = rest[:-1], rest[-1]
    tm = x_ref.shape[1]
    mod = mod_ref[0]
    h = (x_ref[0] * (1.0 + mod[1:2]) + mod[0:1]).astype(BF16)
    for j in range(6):
        sl = slice(j * HALF_D, (j + 1) * HALF_D)
        acc = jnp.dot(h, w_ref[:, sl], preferred_element_type=F32) * cs_ref[:, sl]
        o_ref[0, :, sl] = acc.astype(BF16)
        if j >= 3 and dils:
            for c in range(HALF_D // LANES):
                stage_ref[(j - 3) * (HALF_D // LANES) + c] = acc[:, c * LANES:(c + 1) * LANES]
    for dil, p_ref in zip(dils, perm_refs):
        for r in range(dil):
            for c in range(stage_ref.shape[0]):
                rows = stage_ref[c, pl.ds(r, tm // dil, stride=dil), :]
                p_ref[0, r, :, c * LANES:(c + 1) * LANES] = rows.astype(BF16)


def _qkv(x, mod, w, colscale, dils, tm=512):
    bsz, s_len, _ = x.shape
    n = w.shape[1]
    nb = n // 2
    out_specs = [pl.BlockSpec((1, tm, n), lambda b, i: (b, i, 0))]
    out_shape = [jax.ShapeDtypeStruct((bsz, s_len, n), BF16)]
    for dil in dils:
        out_specs.append(pl.BlockSpec((1, dil, tm // dil, nb), lambda b, i: (b, 0, i, 0)))
        out_shape.append(jax.ShapeDtypeStruct((bsz, dil, s_len // dil, nb), BF16))
    return pl.pallas_call(
        functools.partial(_qkv_kernel, dils=dils),
        grid=(bsz, s_len // tm),
        in_specs=[pl.BlockSpec((1, tm, D_MODEL), lambda b, i: (b, i, 0)),
                  pl.BlockSpec((1, 6, D_MODEL), lambda b, i: (b, 0, 0)),
                  pl.BlockSpec((D_MODEL, n), lambda b, i: (0, 0)),
                  pl.BlockSpec((1, n), lambda b, i: (0, 0))],
        out_specs=out_specs,
        out_shape=out_shape,
        scratch_shapes=[pltpu.VMEM((nb // LANES, tm, LANES), F32)],
        compiler_params=_params(("parallel", "parallel")),
        name="qkv",
    )(x, mod, w, colscale)


def _diff_kernel(slope_ref, lam_ref, subw_ref, q_ref, k_ref, v_ref, o_ref,
                 acc_ref, m_ref, mask_ref, vat_ref, sa_ref, sb_ref, *, tk, lam_init):
    tq = 2 * tk
    pr = pl.program_id(1)
    qi = pl.program_id(2)
    n_kb = k_ref.shape[1] // tk
    sl2 = [slope_ref[2 * pr + hh] * LOG2E for hh in range(2)]

    @pl.when(qi == 0)
    def _():
        kr = lax.broadcasted_iota(jnp.int32, (tk, tq), 0)
        qc = lax.broadcasted_iota(jnp.int32, (tk, tq), 1)
        mask_ref[0] = jnp.where(qc >= kr, 0.0, NEG_INF)
        mask_ref[1] = jnp.where(qc - tk >= kr, 0.0, NEG_INF)
        row = lax.broadcasted_iota(jnp.int32, (LANES, tk), 0)
        for j in range(n_kb):
            vt = v_ref[0, j * tk:(j + 1) * tk, :].astype(F32).T
            vat_ref[0, j] = jnp.where(row < HEAD_DIM, vt, 1.0).astype(BF16)
            vat_ref[1, j] = jnp.where(row >= HEAD_DIM, vt, 1.0).astype(BF16)

    lane = lax.broadcasted_iota(jnp.int32, (1, LANES), 1)
    q = q_ref[0]
    feats = []
    for hh in range(2):
        slv = jnp.zeros((1, LANES), F32) + sl2[hh]
        hi = slv.astype(BF16).astype(F32)
        r1 = slv - hi
        lo = r1.astype(BF16).astype(F32)
        lo2 = (r1 - lo).astype(BF16).astype(F32)
        feat = jnp.where(lane == 0, hi, jnp.where(lane == 1, lo, jnp.where(lane == 2, lo2, 0.0)))
        feats.append(jnp.broadcast_to(feat, (tq, LANES)).astype(BF16))
    q_aug = [jnp.concatenate([jnp.where(lane // DIFF_QK_DIM == m, q, jnp.zeros_like(q)), feats[m // 2]], axis=1)
             for m in range(4)]
    key_pos = lax.broadcasted_iota(jnp.int32, (tk, LANES), 0).astype(F32)
    pos_feat = jnp.where(lane < 3, key_pos, 0.0).astype(BF16)
    acc_ref[...] = jnp.zeros_like(acc_ref)
    m_ref[...] = jnp.full_like(m_ref, NEG_INF)

    def scores(j, s_ref):
        off = pl.multiple_of(j * tk, tk)
        k_aug = jnp.concatenate([k_ref[0, pl.ds(off, tk), :], pos_feat], axis=1)
        for m in range(4):
            s_ref[m] = lax.dot_general(k_aug, q_aug[m], _NT, preferred_element_type=F32)

    def softmax_pv(j, s_ref, variant):
        blk_off = (qi * tq - j * tk).astype(F32)
        ps, alphas = [], []
        for m in range(4):
            t = s_ref[m] if variant == 0 else s_ref[m] + mask_ref[variant - 1]
            c = -sl2[m // 2] * blk_off
            m_old = m_ref[m]
            m_new = jnp.maximum(m_old, jnp.max(t, axis=0, keepdims=True) + c)
            alphas.append(jnp.exp2(m_old - m_new))
            ps.append(jnp.exp2(t - (m_new - c)).astype(BF16))
            m_ref[m] = m_new
        for m in range(4):
            acc_ref[m] = alphas[m] * acc_ref[m] + jnp.dot(vat_ref[m // 2, j], ps[m], preferred_element_type=F32)

    scores(0, sa_ref)

    def body(i, carry):
        j = 2 * i
        scores(j + 1, sb_ref)
        softmax_pv(j, sa_ref, 0)
        scores(j + 2, sa_ref)
        softmax_pv(j + 1, sb_ref, 0)
        return carry

    lax.fori_loop(0, qi, body, 0)
    scores(2 * qi + 1, sb_ref)
    softmax_pv(2 * qi, sa_ref, 1)
    softmax_pv(2 * qi + 1, sb_ref, 2)

    lam_v = lam_ref[...]
    lam = (jnp.exp(jnp.sum(lam_v[0:1] * lam_v[1:2], axis=-1, keepdims=True))
           - jnp.exp(jnp.sum(lam_v[2:3] * lam_v[3:4], axis=-1, keepdims=True)) + lam_init)
    normed = []
    for hh in range(2):
        o_rows = slice(hh * HEAD_DIM, (hh + 1) * HEAD_DIM)
        l_row = slice((1 - hh) * HEAD_DIM, (1 - hh) * HEAD_DIM + 1)
        outs = []
        for mm in range(2):
            a = acc_ref[2 * hh + mm]
            outs.append(a[o_rows] / a[l_row])
        d = outs[0] - lam * outs[1]
        ms = jnp.mean(d * d, axis=0, keepdims=True)
        normed.append(d * lax.rsqrt(ms + EPS))
    y_t = jnp.concatenate(normed, axis=0) * subw_ref[...] * (1.0 - lam_init)
    o_ref[0] = y_t.T.astype(BF16)


def _diff_attention(proj, slopes, lam_vecs, subw, layer_idx, tk=256):
    bsz, s_len, _ = proj.shape
    tq = 2 * tk
    n_pairs = N_HEADS_DIFF // 2
    lam_init = 0.8 - 0.6 * math.exp(-0.3 * layer_idx)
    kern = functools.partial(_diff_kernel, tk=tk, lam_init=lam_init)
    return pl.pallas_call(
        kern,
        grid=(bsz, n_pairs, s_len // tq),
        in_specs=[pl.BlockSpec(memory_space=pltpu.SMEM),
                  pl.BlockSpec((4, DIFF_QK_DIM), lambda b, p, i: (0, 0)),
                  pl.BlockSpec((LANES, 1), lambda b, p, i: (0, 0)),
                  pl.BlockSpec((1, tq, LANES), lambda b, p, i: (b, i, p)),
                  pl.BlockSpec((1, s_len, LANES), lambda b, p, i: (b, 0, n_pairs + p)),
                  pl.BlockSpec((1, s_len, LANES), lambda b, p, i: (b, 0, 2 * n_pairs + p))],
        out_specs=pl.BlockSpec((1, tq, LANES), lambda b, p, i: (b, i, p)),
        out_shape=jax.ShapeDtypeStruct((bsz, s_len, HALF_D), BF16),
        scratch_shapes=[pltpu.VMEM((4, LANES, tq), F32),
                        pltpu.VMEM((4, 1, tq), F32),
                        pltpu.VMEM((2, tk, tq), F32),
                        pltpu.VMEM((2, s_len // tk, LANES, tk), BF16),
                        pltpu.VMEM((4, tk, tq), F32),
                        pltpu.VMEM((4, tk, tq), F32)],
        compiler_params=_params(("parallel", "parallel", "arbitrary")),
        name="diff_attn",
    )(slopes, lam_vecs, subw, proj, proj, proj)


def _dil_kernel(q_ref, kp_ref, kc_ref, vp_ref, vc_ref, o_ref, l_ref, bias_ref, *, dil, slopes):
    blk = ATTN_BLOCK
    n = pl.program_id(2)
    first = (pl.program_id(0) == 0) & (pl.program_id(1) == 0) & (n == 0)

    @pl.when(first)
    def _():
        qi = lax.broadcasted_iota(jnp.int32, (blk, 2 * blk), 0)
        kj = lax.broadcasted_iota(jnp.int32, (blk, 2 * blk), 1)
        dist = qi + blk - kj
        valid = (dist >= 0) & (dist <= blk)
        dist_f = (dist * dil).astype(F32)
        for h in range(N_HEADS_DIL):
            bias = (-slopes[h] * LOG2E) * dist_f
            bias_ref[h, 0] = jnp.where(valid, bias, NEG_INF)
            bias_ref[h, 1] = jnp.where(valid & (kj >= blk), bias, NEG_INF)

    sel = (n == 0).astype(jnp.int32)
    lane = lax.broadcasted_iota(jnp.int32, (1, LANES), 1)
    even = lane < HEAD_DIM
    one = jnp.ones((), BF16)
    for p in range(N_HEADS_DIL // 2):
        sl = slice(p * LANES, (p + 1) * LANES)
        qp = q_ref[:, sl]
        kcat = jnp.concatenate([kp_ref[:, sl], kc_ref[:, sl]], axis=0)
        vcat = jnp.concatenate([vp_ref[:, sl], vc_ref[:, sl]], axis=0)
        accs, ms = [], []
        for hh in range(2):
            keep = even if hh == 0 else jnp.logical_not(even)
            qh = jnp.where(keep, qp, jnp.zeros_like(qp))
            va = jnp.where(keep, vcat, one)
            s = lax.dot_general(qh, kcat, _NT, preferred_element_type=F32)
            t = s + bias_ref[2 * p + hh, sel]
            m = jnp.max(t, axis=-1, keepdims=True)
            pm = jnp.exp2(t - m)
            accs.append(jnp.dot(pm.astype(BF16), va, preferred_element_type=F32))
            ms.append(m)
        o_un = jnp.where(even, accs[0], accs[1])
        l_sum = pltpu.roll(jnp.where(even, accs[1], accs[0]), HEAD_DIM, 1)
        m_pair = jnp.where(even, ms[0], ms[1])
        o_ref[:, sl] = o_un / l_sum
        l_ref[:, sl] = m_pair + jnp.log2(l_sum)


def _dilated_branch(qkv_planes, dil, slopes, col0):
    bsz, _, sub_len, _ = qkv_planes.shape
    blk = ATTN_BLOCK
    nb = sub_len // blk
    kern = functools.partial(_dil_kernel, dil=dil, slopes=slopes)

    def cur(col):
        return pl.BlockSpec((None, None, blk, HALF_D), lambda b, r, n: (b, r, n, col0 + col))

    def prev(col):
        return pl.BlockSpec((None, None, blk, HALF_D), lambda b, r, n: (b, r, jnp.maximum(n - 1, 0), col0 + col))

    out_spec = pl.BlockSpec((None, None, blk, HALF_D), lambda b, r, n: (b, r, n, 0))
    return pl.pallas_call(
        kern,
        grid=(bsz, dil, nb),
        in_specs=[cur(0), prev(1), cur(1), prev(2), cur(2)],
        out_specs=[out_spec, out_spec],
        out_shape=[jax.ShapeDtypeStruct((bsz, dil, sub_len, HALF_D), F32)] * 2,
        scratch_shapes=[pltpu.VMEM((N_HEADS_DIL, 2, blk, 2 * blk), F32)],
        compiler_params=_params(("arbitrary", "arbitrary", "arbitrary")),
        name=f"dilated_d{dil}",
    )(*([qkv_planes] * 5))


def _attn_out_kernel(oa_ref, o1_ref, l1_ref, o2_ref, l2_ref, o3_ref, l3_ref, x_ref, mod_ref,
                     w_ref, g_ref, b_ref, out_ref, tok_ref, ob_ref, *, dils):
    tm = x_ref.shape[1]
    branch_refs = ((o1_ref, l1_ref), (o2_ref, l2_ref), (o3_ref, l3_ref))
    for c in range(HALF_D // LANES):
        cols = slice(c * LANES, (c + 1) * LANES)
        vals = []
        for g, (dil, pair) in enumerate(zip(dils, branch_refs)):
            for a, src_ref in enumerate(pair):
                if dil == 1:
                    vals.append(src_ref[0, 0, :, cols])
                else:
                    for r in range(dil):
                        tok_ref[g, a, pl.ds(r, tm // dil, stride=dil), :] = src_ref[0, r, :, cols]
                    vals.append(tok_ref[g, a])
        o1, l1, o2, l2, o3, l3 = vals
        mx = jnp.maximum(jnp.maximum(l1, l2), l3)
        e1, e2, e3 = jnp.exp2(l1 - mx), jnp.exp2(l2 - mx), jnp.exp2(l3 - mx)
        ob_ref[:, cols] = ((e1 * o1 + e2 * o2 + e3 * o3) / (e1 + e2 + e3)).astype(BF16)
    y = (jnp.dot(oa_ref[0], w_ref[:HALF_D], preferred_element_type=F32)
         + jnp.dot(ob_ref[...], w_ref[HALF_D:], preferred_element_type=F32))
    mod = mod_ref[0]
    z = DN_ALPHA * x_ref[0] + (1.0 + mod[2:3]) * y
    out_ref[0] = _layernorm(z, g_ref[...], b_ref[...])


def _attn_out(oa, dil_outs, dils, x, mod, w, ln_g, ln_b, tm=256):
    bsz, s_len, _ = x.shape
    half = pl.BlockSpec((1, tm, HALF_D), lambda b, i: (b, i, 0))
    full = pl.BlockSpec((1, tm, D_MODEL), lambda b, i: (b, i, 0))
    vec = pl.BlockSpec((1, D_MODEL), lambda b, i: (0, 0))
    plane_specs = []
    for dil in dils:
        spec = pl.BlockSpec((1, dil, tm // dil, HALF_D), lambda b, i: (b, 0, i, 0))
        plane_specs += [spec, spec]
    flat = [a for pair in dil_outs for a in pair]
    return pl.pallas_call(
        functools.partial(_attn_out_kernel, dils=dils),
        grid=(bsz, s_len // tm),
        in_specs=[half] + plane_specs + [full, pl.BlockSpec((1, 6, D_MODEL), lambda b, i: (b, 0, 0)),
                                         pl.BlockSpec((D_MODEL, D_MODEL), lambda b, i: (0, 0)), vec, vec],
        out_specs=full,
        out_shape=jax.ShapeDtypeStruct((bsz, s_len, D_MODEL), F32),
        scratch_shapes=[pltpu.VMEM((len(dils), 2, tm, LANES), F32), pltpu.VMEM((tm, HALF_D), BF16)],
        compiler_params=_params(("parallel", "parallel")),
        name="attn_out",
    )(oa, *flat, x, mod, w, ln_g.reshape(1, -1), ln_b.reshape(1, -1))


def _ffn_kernel(*refs, gated, emit_next):
    it = iter(refs)
    x_ref, mod_ref = next(it), next(it)
    cw_ref = next(it) if gated else None
    wg_ref, wu_ref, wd_ref, g_ref, b_ref = next(it), next(it), next(it), next(it), next(it)
    nmod_ref = next(it) if emit_next else None
    out_ref = next(it)
    nxt_ref = next(it) if emit_next else None
    h_ref, acc_ref = next(it), next(it)

    j = pl.program_id(2)
    mod = mod_ref[0]

    @pl.when(j == 0)
    def _():
        h_ref[...] = (x_ref[0] * (1.0 + mod[4:5]) + mod[3:4]).astype(BF16)
        acc_ref[...] = jnp.zeros_like(acc_ref)

    h = h_ref[...]
    wg = wg_ref[0] if gated else wg_ref[...]
    wu = wu_ref[0] if gated else wu_ref[...]
    wd = wd_ref[0] if gated else wd_ref[...]
    gate = jnp.dot(h, wg, preferred_element_type=F32)
    up = jnp.dot(h, wu, preferred_element_type=F32)
    act = (jax.nn.silu(gate) * up).astype(BF16)
    y = jnp.dot(act, wd, preferred_element_type=F32)
    if gated:
        lane = lax.broadcasted_iota(jnp.int32, (1, LANES), 1)
        cw = jnp.sum(jnp.where(lane == j, cw_ref[0], 0.0), axis=-1, keepdims=True)
        y = cw * y
    acc_ref[...] += y

    @pl.when(j == pl.num_programs(2) - 1)
    def _():
        z = DN_ALPHA * x_ref[0] + (1.0 + mod[5:6]) * acc_ref[...]
        out = _layernorm(z, g_ref[...], b_ref[...])
        out_ref[0] = out
        if emit_next:
            nmod = nmod_ref[0]
            nxt_ref[0] = (out * (1.0 + nmod[1:2]) + nmod[0:1]).astype(BF16)


def _ffn(x, mod, wg, wu, wd, ln_g, ln_b, *, cw=None, next_mod=None, tm=512, tf=1408):
    bsz, s_len, _ = x.shape
    gated = cw is not None
    emit_next = next_mod is not None
    full = pl.BlockSpec((1, tm, D_MODEL), lambda b, i, j: (b, i, 0))
    modspec = pl.BlockSpec((1, 6, D_MODEL), lambda b, i, j: (b, 0, 0))
    vec = pl.BlockSpec((1, D_MODEL), lambda b, i, j: (0, 0))
    if gated:
        n_inner = wg.shape[0]
        w_in_spec = pl.BlockSpec((1, D_MODEL, wg.shape[2]), lambda b, i, j: (j, 0, 0))
        w_dn_spec = pl.BlockSpec((1, wd.shape[1], D_MODEL), lambda b, i, j: (j, 0, 0))
    else:
        n_inner = wg.shape[1] // tf
        w_in_spec = pl.BlockSpec((D_MODEL, tf), lambda b, i, j: (0, j))
        w_dn_spec = pl.BlockSpec((tf, D_MODEL), lambda b, i, j: (j, 0))
    in_specs = [full, modspec]
    args = [x, mod]
    if gated:
        in_specs.append(pl.BlockSpec((1, tm, LANES), lambda b, i, j: (b, i, 0)))
        args.append(cw)
    in_specs += [w_in_spec, w_in_spec, w_dn_spec, vec, vec]
    args += [wg, wu, wd, ln_g.reshape(1, -1), ln_b.reshape(1, -1)]
    out_specs = [full]
    out_shape = [jax.ShapeDtypeStruct((bsz, s_len, D_MODEL), F32)]
    if emit_next:
        in_specs.append(modspec)
        args.append(next_mod)
        out_specs.append(full)
        out_shape.append(jax.ShapeDtypeStruct((bsz, s_len, D_MODEL), BF16))
    res = pl.pallas_call(
        functools.partial(_ffn_kernel, gated=gated, emit_next=emit_next),
        grid=(bsz, s_len // tm, n_inner),
        in_specs=in_specs,
        out_specs=out_specs,
        out_shape=out_shape,
        scratch_shapes=[pltpu.VMEM((tm, D_MODEL), BF16), pltpu.VMEM((tm, D_MODEL), F32)],
        compiler_params=_params(("parallel", "parallel", "arbitrary")),
        name="moe_ffn" if gated else "ffn",
    )(*args)
    return res if emit_next else res[0]


def _s5_kernel(ar_r_ref, ai_r_ref, ar_c_ref, ai_c_ref, ldt_ref, brep_re_ref, brep_im_ref,
               bt_re_ref, bt_im_ref, c_re_ref, c_im_ref, u_ref, y_ref, sx_re, sx_im, h_re, h_im):
    sub = SSM_SUB
    gw = SSM_GROUP_WIDTH
    n_lane = sub * gw
    dt = jnp.exp(ldt_ref[0])

    def discretise(ar, ai):
        mag = jnp.exp(dt * ar)
        abr, abi = mag * jnp.cos(dt * ai), mag * jnp.sin(dt * ai)
        den = ar * ar + ai * ai
        nr, ni = abr - 1.0, abi
        return abr, abi, (nr * ar + ni * ai) / den, (ni * ar - nr * ai) / den

    abr, abi, zr, zi = discretise(ar_r_ref[0], ai_r_ref[0])
    _, _, zrc, zic = discretise(ar_c_ref[0], ai_c_ref[0])
    bt_r, bt_i = bt_re_ref[0], bt_im_ref[0]
    bbt_r, bbt_i = zr * bt_r - zi * bt_i, zr * bt_i + zi * bt_r
    brep_r, brep_i = brep_re_ref[0], brep_im_ref[0]
    bbrep_r, bbrep_i = zrc * brep_r - zic * brep_i, zrc * brep_i + zic * brep_r

    pw = []
    pr, pi = jnp.ones_like(abr), jnp.zeros_like(abr)
    for _ in range(sub + 1):
        pw.append((pr, pi))
        pr, pi = pr * abr - pi * abi, pr * abi + pi * abr
    cr, ci = c_re_ref[0], c_im_ref[0]
    m_re = [cr * a - ci * b for a, b in pw]
    m_im = [cr * b + ci * a for a, b in pw]

    kk = (jnp.dot(jnp.concatenate(m_re[:sub], axis=0), bbrep_r, precision=_HI, preferred_element_type=F32)
          - jnp.dot(jnp.concatenate(m_im[:sub], axis=0), bbrep_i, precision=_HI, preferred_element_type=F32))
    lane_slot = lax.broadcasted_iota(jnp.int32, (n_lane, n_lane), 1) // gw
    tt = jnp.zeros((n_lane, n_lane), F32)
    for s in range(sub):
        if s == 0:
            shifted = kk
        else:
            shifted = jnp.concatenate([jnp.zeros((s * gw, n_lane), F32), kk[:n_lane - s * gw]], axis=0)
        tt = jnp.where(lane_slot == s, shifted, tt)

    w_st_r = jnp.concatenate([bbt_r * pw[sub - 1 - s][0] - bbt_i * pw[sub - 1 - s][1] for s in range(sub)], axis=0)
    w_st_i = jnp.concatenate([bbt_r * pw[sub - 1 - s][1] + bbt_i * pw[sub - 1 - s][0] for s in range(sub)], axis=0)
    w_out_r = jnp.concatenate(m_re[1:], axis=0)
    w_out_i = jnp.concatenate(m_im[1:], axis=0)

    u = u_ref[0]
    def pad_lanes(w):
        return jnp.concatenate([w, jnp.zeros_like(w)], axis=1)

    sx_re[...] = jnp.dot(u, pad_lanes(w_st_r).astype(BF16), preferred_element_type=F32)
    sx_im[...] = jnp.dot(u, pad_lanes(w_st_i).astype(BF16), preferred_element_type=F32)

    n_batch = 8
    unroll = 8
    nj = u.shape[0] // n_batch
    alr = jnp.broadcast_to(pad_lanes(pw[sub][0]), (n_batch, LANES))
    ali = jnp.broadcast_to(pad_lanes(pw[sub][1]), (n_batch, LANES))

    def body(i, carry):
        hr, hi = carry
        for k in range(unroll):
            rows_j = pl.ds(i * unroll + k, n_batch, stride=nj)
            h_re[rows_j, :] = hr
            h_im[rows_j, :] = hi
            sr, si = sx_re[rows_j, :], sx_im[rows_j, :]
            hr, hi = alr * hr - ali * hi + sr, alr * hi + ali * hr + si
        return hr, hi

    zero = jnp.zeros((n_batch, LANES), F32)
    lax.fori_loop(0, nj // unroll, body, (zero, zero))

    y = lax.dot_general(u, tt.astype(BF16), _NT, preferred_element_type=F32)
    y += lax.dot_general(h_re[...].astype(BF16), pad_lanes(w_out_r).astype(BF16), _NT, preferred_element_type=F32)
    y -= lax.dot_general(h_im[...].astype(BF16), pad_lanes(w_out_i).astype(BF16), _NT, preferred_element_type=F32)
    y_ref[0] = y


_GROUPS_PER_TILE = LANES // SSM_GROUP_WIDTH
_SLOTS_PER_TILE = LANES // SSM_GROUP_WIDTH


def _s5_pack_kernel(x_ref, o_ref, stage_ref):
    tm = x_ref.shape[1]
    nj = tm // SSM_SUB
    n_tiles = D_MODEL // LANES
    for c in range(n_tiles):
        stage_ref[c] = x_ref[0, :, c * LANES:(c + 1) * LANES].astype(F32)
    slot = lax.broadcasted_iota(jnp.int32, (nj, LANES), 1) // SSM_GROUP_WIDTH

    def per_tile(c, carry):
        xs = [stage_ref[c, pl.ds(s, nj, stride=SSM_SUB), :] for s in range(SSM_SUB)]
        for gl in range(_GROUPS_PER_TILE):
            for h in range(SSM_SUB // _SLOTS_PER_TILE):
                acc = jnp.zeros((nj, LANES), F32)
                for sm in range(_SLOTS_PER_TILE):
                    shift = (SSM_GROUP_WIDTH * (sm - gl)) % LANES
                    acc = jnp.where(slot == sm, pltpu.roll(xs[_SLOTS_PER_TILE * h + sm], shift, 1), acc)
                o_ref[_GROUPS_PER_TILE * c + gl, :, h * LANES:(h + 1) * LANES] = acc.astype(BF16)
        return carry

    lax.fori_loop(0, n_tiles, per_tile, 0)


def _s5_unpack_kernel(y_ref, o_ref, stage_ref):
    nj = y_ref.shape[1]
    n_tiles = D_MODEL // LANES
    grp = lax.broadcasted_iota(jnp.int32, (nj, LANES), 1) // SSM_GROUP_WIDTH

    def per_tile(c, carry):
        for h in range(SSM_SUB // _SLOTS_PER_TILE):
            srcs = [y_ref[_GROUPS_PER_TILE * c + gl, :, h * LANES:(h + 1) * LANES] for gl in range(_GROUPS_PER_TILE)]
            for sm in range(_SLOTS_PER_TILE):
                acc = jnp.zeros((nj, LANES), F32)
                for gl in range(_GROUPS_PER_TILE):
                    shift = (SSM_GROUP_WIDTH * (gl - sm)) % LANES
                    acc = jnp.where(grp == gl, pltpu.roll(srcs[gl], shift, 1), acc)
                stage_ref[c, pl.ds(_SLOTS_PER_TILE * h + sm, nj, stride=SSM_SUB), :] = acc
        return carry

    lax.fori_loop(0, n_tiles, per_tile, 0)
    for c in range(n_tiles):
        o_ref[0, :, c * LANES:(c + 1) * LANES] = stage_ref[c]


def _s5(u, a_re, a_im, log_dt, b_re, b_im, c_re, c_im, tm=512):
    bsz, s_len, _ = u.shape
    assert bsz == 8, "the chunk scan holds one batch per sublane"
    g, p, gw, sub = SSM_GROUPS, SSM_STATE, SSM_GROUP_WIDTH, SSM_SUB
    nj = s_len // sub
    rows = nj * bsz
    n_t = s_len // tm
    tok_spec = pl.BlockSpec((1, tm, D_MODEL), lambda b, i: (b, i, 0))
    grp_spec = pl.BlockSpec((g, tm // sub, sub * gw), lambda b, i: (0, b * n_t + i, 0))
    stage = pltpu.VMEM((D_MODEL // LANES, tm, LANES), F32)
    ur = pl.pallas_call(
        _s5_pack_kernel,
        grid=(bsz, n_t),
        in_specs=[tok_spec],
        out_specs=grp_spec,
        out_shape=jax.ShapeDtypeStruct((g, rows, sub * gw), BF16),
        scratch_shapes=[stage],
        compiler_params=_params(("parallel", "parallel")),
        name="s5_pack",
    )(u)
    grp = lambda *shape: pl.BlockSpec((1,) + shape, lambda i: (i,) + (0,) * len(shape))
    yr = pl.pallas_call(
        _s5_kernel,
        grid=(g,),
        in_specs=[grp(1, p), grp(1, p), grp(p, 1), grp(p, 1), grp(1, 1),
                  grp(p, sub * gw), grp(p, sub * gw), grp(gw, p), grp(gw, p), grp(gw, p), grp(gw, p),
                  grp(rows, sub * gw)],
        out_specs=grp(rows, sub * gw),
        out_shape=jax.ShapeDtypeStruct((g, rows, sub * gw), F32),
        scratch_shapes=[pltpu.VMEM((rows, LANES), F32)] * 4,
        compiler_params=_params(("parallel",)),
        name="s5",
    )(a_re.reshape(g, 1, p), a_im.reshape(g, 1, p), a_re.reshape(g, p, 1), a_im.reshape(g, p, 1),
      log_dt.reshape(g, 1, 1), jnp.tile(b_re, (1, 1, sub)), jnp.tile(b_im, (1, 1, sub)),
      jnp.swapaxes(b_re, 1, 2), jnp.swapaxes(b_im, 1, 2), c_re, c_im, ur)
    return pl.pallas_call(
        _s5_unpack_kernel,
        grid=(bsz, n_t),
        in_specs=[grp_spec],
        out_specs=tok_spec,
        out_shape=jax.ShapeDtypeStruct((bsz, s_len, D_MODEL), F32),
        scratch_shapes=[stage],
        compiler_params=_params(("parallel", "parallel")),
        name="s5_unpack",
    )(yr)


def _glu_kernel(x_ref, y_ref, mod_ref, dskip_ref, w_ref, bias_ref, g_ref, b_ref, out_ref):
    mod = mod_ref[0]
    x = x_ref[0]
    u = x * (1.0 + mod[1:2]) + mod[0:1]
    y = y_ref[0] + dskip_ref[...] * u
    act = jax.nn.gelu(y).astype(BF16)
    z = jnp.dot(act, w_ref[...], preferred_element_type=F32) + bias_ref[...]
    yy = z[:, :D_MODEL] * jax.nn.sigmoid(z[:, D_MODEL:])
    out_ref[0] = _layernorm(DN_ALPHA * x + (1.0 + mod[2:3]) * yy, g_ref[...], b_ref[...])


def _glu(x, y_ssm, mod, d_skip, w, bias, ln_g, ln_b, tm=256):
    bsz, s_len, _ = x.shape
    full = pl.BlockSpec((1, tm, D_MODEL), lambda b, i: (b, i, 0))
    vec = pl.BlockSpec((1, D_MODEL), lambda b, i: (0, 0))
    return pl.pallas_call(
        _glu_kernel,
        grid=(bsz, s_len // tm),
        in_specs=[full, full, pl.BlockSpec((1, 6, D_MODEL), lambda b, i: (b, 0, 0)), vec,
                  pl.BlockSpec((D_MODEL, 2 * D_MODEL), lambda b, i: (0, 0)),
                  pl.BlockSpec((1, 2 * D_MODEL), lambda b, i: (0, 0)), vec, vec],
        out_specs=full,
        out_shape=jax.ShapeDtypeStruct((bsz, s_len, D_MODEL), F32),
        compiler_params=_params(("parallel", "parallel")),
        name="glu",
    )(x, y_ssm, mod, d_skip.reshape(1, -1), w, bias.reshape(1, -1), ln_g.reshape(1, -1), ln_b.reshape(1, -1))


def _router_kernel(x_ref, mod_ref, w_ref, b_ref, cw_ref):
    mod = mod_ref[0]
    h = x_ref[0] * (1.0 + mod[4:5]) + mod[3:4]
    w = w_ref[...]
    h_hi = h.astype(BF16)
    h_lo = (h - h_hi.astype(F32)).astype(BF16)
    w_hi = w.astype(BF16)
    w_lo = (w - w_hi.astype(F32)).astype(BF16)
    logits = (jnp.dot(h_hi, w_hi, preferred_element_type=F32)
              + jnp.dot(h_hi, w_lo, preferred_element_type=F32)
              + jnp.dot(h_lo, w_hi, preferred_element_type=F32)) + b_ref[...]
    lane = lax.broadcasted_iota(jnp.int32, logits.shape, 1).astype(F32)
    logits = jnp.where(lane < N_EXPERTS, logits, NEG_INF)
    v1 = jnp.max(logits, axis=-1, keepdims=True)
    i1 = jnp.min(jnp.where(logits == v1, lane, float(LANES)), axis=-1, keepdims=True)
    rest = jnp.where(lane == i1, NEG_INF, logits)
    v2 = jnp.max(rest, axis=-1, keepdims=True)
    i2 = jnp.min(jnp.where(rest == v2, lane, float(LANES)), axis=-1, keepdims=True)
    e = jnp.exp(v2 - v1)
    g1 = 1.0 / (1.0 + e)
    g2 = e / (1.0 + e)
    cw_ref[0] = jnp.where(lane == i1, g1, 0.0) + jnp.where(lane == i2, g2, 0.0)


def _router(x, mod, rw, rb, tm=512):
    bsz, s_len, _ = x.shape
    rw_pad = jnp.zeros((D_MODEL, LANES), F32).at[:, :N_EXPERTS].set(rw)
    rb_pad = jnp.zeros((1, LANES), F32).at[0, :N_EXPERTS].set(rb)
    return pl.pallas_call(
        _router_kernel,
        grid=(bsz, s_len // tm),
        in_specs=[pl.BlockSpec((1, tm, D_MODEL), lambda b, i: (b, i, 0)),
                  pl.BlockSpec((1, 6, D_MODEL), lambda b, i: (b, 0, 0)),
                  pl.BlockSpec((D_MODEL, LANES), lambda b, i: (0, 0)),
                  pl.BlockSpec((1, LANES), lambda b, i: (0, 0))],
        out_specs=pl.BlockSpec((1, tm, LANES), lambda b, i: (b, i, 0)),
        out_shape=jax.ShapeDtypeStruct((bsz, s_len, LANES), F32),
        compiler_params=_params(("parallel", "parallel")),
        name="router",
    )(x, mod, rw_pad, rb_pad)


def _alibi_slopes():
    i = jnp.arange(N_ATTN_HEADS, dtype=F32) + 1.0
    return jnp.exp2(-8.0 * i / N_ATTN_HEADS)


def kernel(x, c, l0_ada_w, l0_ada_b, l0_w_in, l0_lam_q1, l0_lam_k1, l0_lam_q2, l0_lam_k2, l0_subln_w, l0_w_out, l0_ln1_g, l0_ln1_b, l0_ffn_w_gate, l0_ffn_w_up, l0_ffn_w_down, l0_ln2_g, l0_ln2_b, l1_ada_w, l1_ada_b, l1_a_re, l1_a_im, l1_log_dt, l1_b_re, l1_b_im, l1_c_re, l1_c_im, l1_d_skip, l1_w_glu, l1_b_glu, l1_ln1_g, l1_ln1_b, l1_router_w, l1_router_b, l1_exp_w_gate, l1_exp_w_up, l1_exp_w_down, l1_ln2_g, l1_ln2_b):
    mod0 = _ada(c, l0_ada_w, l0_ada_b)
    mod1 = _ada(c, l1_ada_w, l1_ada_b)

    q_a = jnp.full((HALF_D,), DIFF_QK_DIM ** -0.5 * LOG2E, F32)
    q_b = jnp.full((HALF_D,), HEAD_DIM ** -0.5 * LOG2E, F32)
    ones = jnp.ones((D_MODEL,), F32)
    colscale = jnp.concatenate([q_a, ones, q_b, ones]).reshape(1, -1)
    dils = tuple(dil for _, dil in DIL_PAIRS)
    proj, *planes = _qkv(x, mod0, l0_w_in.astype(BF16), colscale, dils[1:])
    slopes = _alibi_slopes()
    lam_vecs = jnp.stack([l0_lam_q1, l0_lam_k1, l0_lam_q2, l0_lam_k2]).astype(F32)
    subw = jnp.tile(l0_subln_w.astype(F32), 2).reshape(LANES, 1)
    oa = _diff_attention(proj, slopes[:N_HEADS_DIFF], lam_vecs, subw, 0)
    dil_slopes = tuple(2.0 ** (-8.0 * (h + 1.0) / N_ATTN_HEADS) for h in range(N_HEADS_DIFF, N_ATTN_HEADS))
    assert dils[0] == 1
    dil_outs = [_dilated_branch(proj[:, None], 1, dil_slopes, 3)]
    dil_outs += [_dilated_branch(pln, dil, dil_slopes, 0) for pln, dil in zip(planes, dils[1:])]
    x = _attn_out(oa, dil_outs, dils, x, mod0, l0_w_out.astype(BF16), l0_ln1_g, l0_ln1_b)
    x, u = _ffn(x, mod0, l0_ffn_w_gate.astype(BF16), l0_ffn_w_up.astype(BF16), l0_ffn_w_down.astype(BF16),
                l0_ln2_g, l0_ln2_b, next_mod=mod1)

    y_ssm = _s5(u, l1_a_re, l1_a_im, l1_log_dt, l1_b_re, l1_b_im, l1_c_re, l1_c_im)
    x = _glu(x, y_ssm, mod1, l1_d_skip, l1_w_glu.astype(BF16), l1_b_glu, l1_ln1_g, l1_ln1_b)
    cw = _router(x, mod1, l1_router_w, l1_router_b)
    x = _ffn(x, mod1, l1_exp_w_gate.astype(BF16), l1_exp_w_up.astype(BF16), l1_exp_w_down.astype(BF16),
             l1_ln2_g, l1_ln2_b, cw=cw)
    return x
```

```python
import functools
import math

import jax
import jax.numpy as jnp
from jax import lax
from jax.experimental import pallas as pl
from jax.experimental.pallas import tpu as pltpu

F32 = jnp.float32
BF16 = jnp.bfloat16

D_MODEL = 1024
DEPTH = 2
HEAD_DIM = 64
N_HEADS_DIFF = D_MODEL // (2 * HEAD_DIM)
DIFF_QK_DIM = HEAD_DIM // 2
N_HEADS_DIL = D_MODEL // (2 * HEAD_DIM)
N_ATTN_HEADS = N_HEADS_DIFF + N_HEADS_DIL
DIL_PAIRS = ((128, 1), (512, 4), (2048, 16))
ATTN_BLOCK = 128
SSM_GROUP_WIDTH = 16
SSM_GROUPS = D_MODEL // SSM_GROUP_WIDTH
SSM_STATE = 64
SSM_SUB = 16
D_FF = 2816
N_EXPERTS = 8
D_FF_EXPERT = 1408
DN_ALPHA = (2.0 * DEPTH) ** 0.25
EPS = 1e-5
LOG2E = 1.4426950408889634
LANES = 128
HALF_D = D_MODEL // 2
NEG_INF = float("-inf")

_NT = (((1,), (1,)), ((), ()))
_HI = lax.Precision.HIGHEST


def _params(sem, vmem_mb=48):
    return pltpu.CompilerParams(dimension_semantics=sem, vmem_limit_bytes=vmem_mb * 1024 * 1024)


def _layernorm(z, g, b):
    mu = jnp.mean(z, axis=-1, keepdims=True)
    zc = z - mu
    var = jnp.mean(zc * zc, axis=-1, keepdims=True)
    return zc * lax.rsqrt(var + EPS) * g + b


def _ada_kernel(c_ref, w_ref, b_ref, o_ref):
    h = jax.nn.silu(c_ref[...])
    o_ref[...] = jnp.dot(h.astype(BF16), w_ref[...].astype(BF16),
                         preferred_element_type=F32) + b_ref[...]


def _ada(c, w, b):
    bsz = c.shape[0]
    n = w.shape[1]
    tn = 1024
    out = pl.pallas_call(
        _ada_kernel,
        grid=(n // tn,),
        in_specs=[pl.BlockSpec((bsz, D_MODEL), lambda j: (0, 0)),
                  pl.BlockSpec((D_MODEL, tn), lambda j: (0, j)),
                  pl.BlockSpec((1, tn), lambda j: (0, j))],
        out_specs=pl.BlockSpec((bsz, tn), lambda j: (0, j)),
        out_shape=jax.ShapeDtypeStruct((bsz, n), F32),
        compiler_params=_params(("arbitrary",)),
        name="ada",
    )(c, w, b.reshape(1, n))
    return out.reshape(bsz, 6, D_MODEL)


def _qkv_kernel(x_ref, mod_ref, w_ref, cs_ref, o_ref, *rest, dils):
    perm_refs, stage_ref = rest[:-1], rest[-1]
    tm = x_ref.shape[1]
    mod = mod_ref[0]
    h = (x_ref[0] * (1.0 + mod[1:2]) + mod[0:1]).astype(BF16)
    for j in range(6):
        sl = slice(j * HALF_D, (j + 1) * HALF_D)
        acc = jnp.dot(h, w_ref[:, sl], preferred_element_type=F32) * cs_ref[:, sl]
        o_ref[0, :, sl] = acc.astype(BF16)
        if j >= 3 and dils:
            for c in range(HALF_D // LANES):
                stage_ref[(j - 3) * (HALF_D // LANES) + c] = acc[:, c * LANES:(c + 1) * LANES]
    for dil, p_ref in zip(dils, perm_refs):
        for r in range(dil):
            for c in range(stage_ref.shape[0]):
                rows = stage_ref[c, pl.ds(r, tm // dil, stride=dil), :]
                p_ref[0, r, :, c * LANES:(c + 1) * LANES] = rows.astype(BF16)


def _qkv(x, mod, w, colscale, dils, tm=512):
    bsz, s_len, _ = x.shape
    n = w.shape[1]
    nb = n // 2
    out_specs = [pl.BlockSpec((1, tm, n), lambda b, i: (b, i, 0))]
    out_shape = [jax.ShapeDtypeStruct((bsz, s_len, n), BF16)]
    for dil in dils:
        out_specs.append(pl.BlockSpec((1, dil, tm // dil, nb), lambda b, i: (b, 0, i, 0)))
        out_shape.append(jax.ShapeDtypeStruct((bsz, dil, s_len // dil, nb), BF16))
    return pl.pallas_call(
        functools.partial(_qkv_kernel, dils=dils),
        grid=(bsz, s_len // tm),
        in_specs=[pl.BlockSpec((1, tm, D_MODEL), lambda b, i: (b, i, 0)),
                  pl.BlockSpec((1, 6, D_MODEL), lambda b, i: (b, 0, 0)),
                  pl.BlockSpec((D_MODEL, n), lambda b, i: (0, 0)),
                  pl.BlockSpec((1, n), lambda b, i: (0, 0))],
        out_specs=out_specs,
        out_shape=out_shape,
        scratch_shapes=[pltpu.VMEM((nb // LANES, tm, LANES), F32)],
        compiler_params=_params(("parallel", "parallel")),
        name="qkv",
    )(x, mod, w, colscale)


def _diff_kernel(slope_ref, lam_ref, subw_ref, q_ref, k_ref, v_ref, o_ref,
                 acc_ref, m_ref, mask_ref, vat_ref, sa_ref, sb_ref, *, tk, lam_init):
    tq = 2 * tk
    pr = pl.program_id(1)
    qi = pl.program_id(2)
    n_kb = k_ref.shape[1] // tk
    sl2 = [slope_ref[2 * pr + hh] * LOG2E for hh in range(2)]

    @pl.when(qi == 0)
    def _():
        kr = lax.broadcasted_iota(jnp.int32, (tk, tq), 0)
        qc = lax.broadcasted_iota(jnp.int32, (tk, tq), 1)
        mask_ref[0] = jnp.where(qc >= kr, 0.0, NEG_INF)
        mask_ref[1] = jnp.where(qc - tk >= kr, 0.0, NEG_INF)
        row = lax.broadcasted_iota(jnp.int32, (LANES, tk), 0)
        for j in range(n_kb):
            vt = v_ref[0, j * tk:(j + 1) * tk, :].astype(F32).T
            vat_ref[0, j] = jnp.where(row < HEAD_DIM, vt, 1.0).astype(BF16)
            vat_ref[1, j] = jnp.where(row >= HEAD_DIM, vt, 1.0).astype(BF16)

    lane = lax.broadcasted_iota(jnp.int32, (1, LANES), 1)
    q = q_ref[0]
    feats = []
    for hh in range(2):
        slv = jnp.zeros((1, LANES), F32) + sl2[hh]
        hi = slv.astype(BF16).astype(F32)
        r1 = slv - hi
        lo = r1.astype(BF16).astype(F32)
        lo2 = (r1 - lo).astype(BF16).astype(F32)
        feat = jnp.where(lane == 0, hi, jnp.where(lane == 1, lo, jnp.where(lane == 2, lo2, 0.0)))
        feats.append(jnp.broadcast_to(feat, (tq, LANES)).astype(BF16))
    q_aug = [jnp.concatenate([jnp.where(lane // DIFF_QK_DIM == m, q, jnp.zeros_like(q)), feats[m // 2]], axis=1)
             for m in range(4)]
    key_pos = lax.broadcasted_iota(jnp.int32, (tk, LANES), 0).astype(F32)
    pos_feat = jnp.where(lane < 3, key_pos, 0.0).astype(BF16)
    acc_ref[...] = jnp.zeros_like(acc_ref)
    m_ref[...] = jnp.full_like(m_ref, NEG_INF)

    def scores(j, s_ref):
        off = pl.multiple_of(j * tk, tk)
        k_aug = jnp.concatenate([k_ref[0, pl.ds(off, tk), :], pos_feat], axis=1)
        for m in range(4):
            s_ref[m] = lax.dot_general(k_aug, q_aug[m], _NT, preferred_element_type=F32)

    def softmax_pv(j, s_ref, variant):
        blk_off = (qi * tq - j * tk).astype(F32)
        ps, alphas = [], []
        for m in range(4):
            t = s_ref[m] if variant == 0 else s_ref[m] + mask_ref[variant - 1]
            c = -sl2[m // 2] * blk_off
            m_old = m_ref[m]
            m_new = jnp.maximum(m_old, jnp.max(t, axis=0, keepdims=True) + c)
            alphas.append(jnp.exp2(m_old - m_new))
            ps.append(jnp.exp2(t - (m_new - c)).astype(BF16))
            m_ref[m] = m_new
        for m in range(4):
            acc_ref[m] = alphas[m] * acc_ref[m] + jnp.dot(vat_ref[m // 2, j], ps[m], preferred_element_type=F32)

    scores(0, sa_ref)

    def body(i, carry):
        j = 2 * i
        scores(j + 1, sb_ref)
        softmax_pv(j, sa_ref, 0)
        scores(j + 2, sa_ref)
        softmax_pv(j + 1, sb_ref, 0)
        return carry

    lax.fori_loop(0, qi, body, 0)
    scores(2 * qi + 1, sb_ref)
    softmax_pv(2 * qi, sa_ref, 1)
    softmax_pv(2 * qi + 1, sb_ref, 2)

    lam_v = lam_ref[...]
    lam = (jnp.exp(jnp.sum(lam_v[0:1] * lam_v[1:2], axis=-1, keepdims=True))
           - jnp.exp(jnp.sum(lam_v[2:3] * lam_v[3:4], axis=-1, keepdims=True)) + lam_init)
    normed = []
    for hh in range(2):
        o_rows = slice(hh * HEAD_DIM, (hh + 1) * HEAD_DIM)
        l_row = slice((1 - hh) * HEAD_DIM, (1 - hh) * HEAD_DIM + 1)
        outs = []
        for mm in range(2):
            a = acc_ref[2 * hh + mm]
            outs.append(a[o_rows] / a[l_row])
        d = outs[0] - lam * outs[1]
        ms = jnp.mean(d * d, axis=0, keepdims=True)
        normed.append(d * lax.rsqrt(ms + EPS))
    y_t = jnp.concatenate(normed, axis=0) * subw_ref[...] * (1.0 - lam_init)
    o_ref[0] = y_t.T.astype(BF16)


def _diff_attention(proj, slopes, lam_vecs, subw, layer_idx, tk=256):
    bsz, s_len, _ = proj.shape
    tq = 2 * tk
    n_pairs = N_HEADS_DIFF // 2
    lam_init = 0.8 - 0.6 * math.exp(-0.3 * layer_idx)
    kern = functools.partial(_diff_kernel, tk=tk, lam_init=lam_init)
    return pl.pallas_call(
        kern,
        grid=(bsz, n_pairs, s_len // tq),
        in_specs=[pl.BlockSpec(memory_space=pltpu.SMEM),
                  pl.BlockSpec((4, DIFF_QK_DIM), lambda b, p, i: (0, 0)),
                  pl.BlockSpec((LANES, 1), lambda b, p, i: (0, 0)),
                  pl.BlockSpec((1, tq, LANES), lambda b, p, i: (b, i, p)),
                  pl.BlockSpec((1, s_len, LANES), lambda b, p, i: (b, 0, n_pairs + p)),
                  pl.BlockSpec((1, s_len, LANES), lambda b, p, i: (b, 0, 2 * n_pairs + p))],
        out_specs=pl.BlockSpec((1, tq, LANES), lambda b, p, i: (b, i, p)),
        out_shape=jax.ShapeDtypeStruct((bsz, s_len, HALF_D), BF16),
        scratch_shapes=[pltpu.VMEM((4, LANES, tq), F32),
                        pltpu.VMEM((4, 1, tq), F32),
                        pltpu.VMEM((2, tk, tq), F32),
                        pltpu.VMEM((2, s_len // tk, LANES, tk), BF16),
                        pltpu.VMEM((4, tk, tq), F32),
                        pltpu.VMEM((4, tk, tq), F32)],
        compiler_params=_params(("parallel", "parallel", "arbitrary")),
        name="diff_attn",
    )(slopes, lam_vecs, subw, proj, proj, proj)


def _dil_kernel(q_ref, kp_ref, kc_ref, vp_ref, vc_ref, o_ref, l_ref, bias_ref, *, dil, slopes):
    blk = ATTN_BLOCK
    n = pl.program_id(2)
    first = (pl.program_id(0) == 0) & (pl.program_id(1) == 0) & (n == 0)

    @pl.when(first)
    def _():
        qi = lax.broadcasted_iota(jnp.int32, (blk, 2 * blk), 0)
        kj = lax.broadcasted_iota(jnp.int32, (blk, 2 * blk), 1)
        dist = qi + blk - kj
        valid = (dist >= 0) & (dist <= blk)
        dist_f = (dist * dil).astype(F32)
        for h in range(N_HEADS_DIL):
            bias = (-slopes[h] * LOG2E) * dist_f
            bias_ref[h, 0] = jnp.where(valid, bias, NEG_INF)
            bias_ref[h, 1] = jnp.where(valid & (kj >= blk), bias, NEG_INF)

    first_sel = (n == 0).astype(jnp.int32)
    lane = lax.broadcasted_iota(jnp.int32, (1, LANES), 1)
    even = lane < HEAD_DIM
    one = jnp.ones((), BF16)
    for i in range(q_ref.shape[0] // blk):
        rows = slice(i * blk, (i + 1) * blk)
        before = slice((i - 1) * blk, i * blk)
        sel = first_sel if i == 0 else 0
        for p in range(N_HEADS_DIL // 2):
            sl = slice(p * LANES, (p + 1) * LANES)
            qp = q_ref[rows, sl]
            k_prev = kp_ref[:, sl] if i == 0 else kc_ref[before, sl]
            v_prev = vp_ref[:, sl] if i == 0 else vc_ref[before, sl]
            kcat = jnp.concatenate([k_prev, kc_ref[rows, sl]], axis=0)
            vcat = jnp.concatenate([v_prev, vc_ref[rows, sl]], axis=0)
            accs, ms = [], []
            for hh in range(2):
                keep = even if hh == 0 else jnp.logical_not(even)
                qh = jnp.where(keep, qp, jnp.zeros_like(qp))
                va = jnp.where(keep, vcat, one)
                s = lax.dot_general(qh, kcat, _NT, preferred_element_type=F32)
                t = s + bias_ref[2 * p + hh, sel]
                m = jnp.max(t, axis=-1, keepdims=True)
                pm = jnp.exp2(t - m)
                accs.append(jnp.dot(pm.astype(BF16), va, preferred_element_type=F32))
                ms.append(m)
            o_un = jnp.where(even, accs[0], accs[1])
            l_sum = pltpu.roll(jnp.where(even, accs[1], accs[0]), HEAD_DIM, 1)
            m_pair = jnp.where(even, ms[0], ms[1])
            o_ref[rows, sl] = o_un / l_sum
            l_ref[rows, sl] = m_pair + jnp.log2(l_sum)


_DIL_BLOCKS_PER_STEP = 4


def _dilated_branch(qkv_planes, dil, slopes, col0):
    bsz, _, sub_len, _ = qkv_planes.shape
    blk = ATTN_BLOCK
    nb = sub_len // blk
    qb = min(_DIL_BLOCKS_PER_STEP, nb)
    kern = functools.partial(_dil_kernel, dil=dil, slopes=slopes)

    def cur(col):
        return pl.BlockSpec((None, None, qb * blk, HALF_D), lambda b, r, n: (b, r, n, col0 + col))

    def prev(col):
        return pl.BlockSpec((None, None, blk, HALF_D),
                            lambda b, r, n: (b, r, jnp.maximum(n * qb - 1, 0), col0 + col))

    out_spec = pl.BlockSpec((None, None, qb * blk, HALF_D), lambda b, r, n: (b, r, n, 0))
    return pl.pallas_call(
        kern,
        grid=(bsz, dil, nb // qb),
        in_specs=[cur(0), prev(1), cur(1), prev(2), cur(2)],
        out_specs=[out_spec, out_spec],
        out_shape=[jax.ShapeDtypeStruct((bsz, dil, sub_len, HALF_D), F32)] * 2,
        scratch_shapes=[pltpu.VMEM((N_HEADS_DIL, 2, blk, 2 * blk), F32)],
        compiler_params=_params(("arbitrary", "arbitrary", "arbitrary")),
        name=f"dilated_d{dil}",
    )(*([qkv_planes] * 5))


def _attn_out_kernel(oa_ref, o1_ref, l1_ref, o2_ref, l2_ref, o3_ref, l3_ref, x_ref, mod_ref,
                     w_ref, g_ref, b_ref, out_ref, tok_ref, ob_ref, *, dils):
    tm = x_ref.shape[1]
    branch_refs = ((o1_ref, l1_ref), (o2_ref, l2_ref), (o3_ref, l3_ref))
    for c in range(HALF_D // LANES):
        cols = slice(c * LANES, (c + 1) * LANES)
        vals = []
        for g, (dil, pair) in enumerate(zip(dils, branch_refs)):
            for a, src_ref in enumerate(pair):
                if dil == 1:
                    vals.append(src_ref[0, 0, :, cols])
                else:
                    for r in range(dil):
                        tok_ref[g, a, pl.ds(r, tm // dil, stride=dil), :] = src_ref[0, r, :, cols]
                    vals.append(tok_ref[g, a])
        o1, l1, o2, l2, o3, l3 = vals
        mx = jnp.maximum(jnp.maximum(l1, l2), l3)
        e1, e2, e3 = jnp.exp2(l1 - mx), jnp.exp2(l2 - mx), jnp.exp2(l3 - mx)
        ob_ref[:, cols] = ((e1 * o1 + e2 * o2 + e3 * o3) / (e1 + e2 + e3)).astype(BF16)
    y = (jnp.dot(oa_ref[0], w_ref[:HALF_D], preferred_element_type=F32)
         + jnp.dot(ob_ref[...], w_ref[HALF_D:], preferred_element_type=F32))
    mod = mod_ref[0]
    z = DN_ALPHA * x_ref[0] + (1.0 + mod[2:3]) * y
    out_ref[0] = _layernorm(z, g_ref[...], b_ref[...])


def _attn_out(oa, dil_outs, dils, x, mod, w, ln_g, ln_b, tm=256):
    bsz, s_len, _ = x.shape
    half = pl.BlockSpec((1, tm, HALF_D), lambda b, i: (b, i, 0))
    full = pl.BlockSpec((1, tm, D_MODEL), lambda b, i: (b, i, 0))
    vec = pl.BlockSpec((1, D_MODEL), lambda b, i: (0, 0))
    plane_specs = []
    for dil in dils:
        spec = pl.BlockSpec((1, dil, tm // dil, HALF_D), lambda b, i: (b, 0, i, 0))
        plane_specs += [spec, spec]
    flat = [a for pair in dil_outs for a in pair]
    return pl.pallas_call(
        functools.partial(_attn_out_kernel, dils=dils),
        grid=(bsz, s_len // tm),
        in_specs=[half] + plane_specs + [full, pl.BlockSpec((1, 6, D_MODEL), lambda b, i: (b, 0, 0)),
                                         pl.BlockSpec((D_MODEL, D_MODEL), lambda b, i: (0, 0)), vec, vec],
        out_specs=full,
        out_shape=jax.ShapeDtypeStruct((bsz, s_len, D_MODEL), F32),
        scratch_shapes=[pltpu.VMEM((len(dils), 2, tm, LANES), F32), pltpu.VMEM((tm, HALF_D), BF16)],
        compiler_params=_params(("parallel", "parallel")),
        name="attn_out",
    )(oa, *flat, x, mod, w, ln_g.reshape(1, -1), ln_b.reshape(1, -1))


def _ffn_kernel(*refs, gated, emit_next):
    it = iter(refs)
    x_ref, mod_ref = next(it), next(it)
    cw_ref = next(it) if gated else None
    wg_ref, wu_ref, wd_ref, g_ref, b_ref = next(it), next(it), next(it), next(it), next(it)
    nmod_ref = next(it) if emit_next else None
    out_ref = next(it)
    nxt_ref = next(it) if emit_next else None
    h_ref, acc_ref = next(it), next(it)

    j = pl.program_id(2)
    mod = mod_ref[0]

    @pl.when(j == 0)
    def _():
        h_ref[...] = (x_ref[0] * (1.0 + mod[4:5]) + mod[3:4]).astype(BF16)
        acc_ref[...] = jnp.zeros_like(acc_ref)

    h = h_ref[...]
    wg = wg_ref.at[0] if gated else wg_ref
    wu = wu_ref.at[0] if gated else wu_ref
    wd = wd_ref.at[0] if gated else wd_ref
    width = wg.shape[1]
    y = None
    for lo in range(0, width, _FF_CHUNK):
        hi = min(lo + _FF_CHUNK, width)
        gate = jnp.dot(h, wg[:, lo:hi], preferred_element_type=F32)
        up = jnp.dot(h, wu[:, lo:hi], preferred_element_type=F32)
        act = (jax.nn.silu(gate) * up).astype(BF16)
        part = jnp.dot(act, wd[lo:hi, :], preferred_element_type=F32)
        y = part if y is None else y + part
    if gated:
        lane = lax.broadcasted_iota(jnp.int32, (1, LANES), 1)
        cw = jnp.sum(jnp.where(lane == j, cw_ref[0], 0.0), axis=-1, keepdims=True)
        y = cw * y
    acc_ref[...] += y

    @pl.when(j == pl.num_programs(2) - 1)
    def _():
        z = DN_ALPHA * x_ref[0] + (1.0 + mod[5:6]) * acc_ref[...]
        out = _layernorm(z, g_ref[...], b_ref[...])
        out_ref[0] = out
        if emit_next:
            nmod = nmod_ref[0]
            nxt_ref[0] = (out * (1.0 + nmod[1:2]) + nmod[0:1]).astype(BF16)


_FF_CHUNK = 768


def _ffn(x, mod, wg, wu, wd, ln_g, ln_b, *, cw=None, next_mod=None, tm=1024, tf=1408):
    bsz, s_len, _ = x.shape
    gated = cw is not None
    emit_next = next_mod is not None
    full = pl.BlockSpec((1, tm, D_MODEL), lambda b, i, j: (b, i, 0))
    modspec = pl.BlockSpec((1, 6, D_MODEL), lambda b, i, j: (b, 0, 0))
    vec = pl.BlockSpec((1, D_MODEL), lambda b, i, j: (0, 0))
    if gated:
        n_inner = wg.shape[0]
        w_in_spec = pl.BlockSpec((1, D_MODEL, wg.shape[2]), lambda b, i, j: (j, 0, 0))
        w_dn_spec = pl.BlockSpec((1, wd.shape[1], D_MODEL), lambda b, i, j: (j, 0, 0))
    else:
        n_inner = wg.shape[1] // tf
        w_in_spec = pl.BlockSpec((D_MODEL, tf), lambda b, i, j: (0, j))
        w_dn_spec = pl.BlockSpec((tf, D_MODEL), lambda b, i, j: (j, 0))
    in_specs = [full, modspec]
    args = [x, mod]
    if gated:
        in_specs.append(pl.BlockSpec((1, tm, LANES), lambda b, i, j: (b, i, 0)))
        args.append(cw)
    in_specs += [w_in_spec, w_in_spec, w_dn_spec, vec, vec]
    args += [wg, wu, wd, ln_g.reshape(1, -1), ln_b.reshape(1, -1)]
    out_specs = [full]
    out_shape = [jax.ShapeDtypeStruct((bsz, s_len, D_MODEL), F32)]
    if emit_next:
        in_specs.append(modspec)
        args.append(next_mod)
        out_specs.append(full)
        out_shape.append(jax.ShapeDtypeStruct((bsz, s_len, D_MODEL), BF16))
    res = pl.pallas_call(
        functools.partial(_ffn_kernel, gated=gated, emit_next=emit_next),
        grid=(bsz, s_len // tm, n_inner),
        in_specs=in_specs,
        out_specs=out_specs,
        out_shape=out_shape,
        scratch_shapes=[pltpu.VMEM((tm, D_MODEL), BF16), pltpu.VMEM((tm, D_MODEL), F32)],
        compiler_params=_params(("parallel", "parallel", "arbitrary"), vmem_mb=56),
        name="moe_ffn" if gated else "ffn",
    )(*args)
    return res if emit_next else res[0]


def _s5_kernel(ar_r_ref, ai_r_ref, ar_c_ref, ai_c_ref, ldt_ref, brep_re_ref, brep_im_ref,
               bt_re_ref, bt_im_ref, c_re_ref, c_im_ref, u_ref, y_ref, sx_re, sx_im, h_re, h_im):
    sub = SSM_SUB
    gw = SSM_GROUP_WIDTH
    n_lane = sub * gw
    dt = jnp.exp(ldt_ref[0])

    def discretise(ar, ai):
        mag = jnp.exp(dt * ar)
        abr, abi = mag * jnp.cos(dt * ai), mag * jnp.sin(dt * ai)
        den = ar * ar + ai * ai
        nr, ni = abr - 1.0, abi
        return abr, abi, (nr * ar + ni * ai) / den, (ni * ar - nr * ai) / den

    abr, abi, zr, zi = discretise(ar_r_ref[0], ai_r_ref[0])
    _, _, zrc, zic = discretise(ar_c_ref[0], ai_c_ref[0])
    bt_r, bt_i = bt_re_ref[0], bt_im_ref[0]
    bbt_r, bbt_i = zr * bt_r - zi * bt_i, zr * bt_i + zi * bt_r
    brep_r, brep_i = brep_re_ref[0], brep_im_ref[0]
    bbrep_r, bbrep_i = zrc * brep_r - zic * brep_i, zrc * brep_i + zic * brep_r

    pw = []
    pr, pi = jnp.ones_like(abr), jnp.zeros_like(abr)
    for _ in range(sub + 1):
        pw.append((pr, pi))
        pr, pi = pr * abr - pi * abi, pr * abi + pi * abr
    cr, ci = c_re_ref[0], c_im_ref[0]
    m_re = [cr * a - ci * b for a, b in pw]
    m_im = [cr * b + ci * a for a, b in pw]

    kk = (jnp.dot(jnp.concatenate(m_re[:sub], axis=0), bbrep_r, precision=_HI, preferred_element_type=F32)
          - jnp.dot(jnp.concatenate(m_im[:sub], axis=0), bbrep_i, precision=_HI, preferred_element_type=F32))
    lane_slot = lax.broadcasted_iota(jnp.int32, (n_lane, n_lane), 1) // gw
    tt = jnp.zeros((n_lane, n_lane), F32)
    for s in range(sub):
        if s == 0:
            shifted = kk
        else:
            shifted = jnp.concatenate([jnp.zeros((s * gw, n_lane), F32), kk[:n_lane - s * gw]], axis=0)
        tt = jnp.where(lane_slot == s, shifted, tt)

    w_st_r = jnp.concatenate([bbt_r * pw[sub - 1 - s][0] - bbt_i * pw[sub - 1 - s][1] for s in range(sub)], axis=0)
    w_st_i = jnp.concatenate([bbt_r * pw[sub - 1 - s][1] + bbt_i * pw[sub - 1 - s][0] for s in range(sub)], axis=0)
    w_out_r = jnp.concatenate(m_re[1:], axis=0)
    w_out_i = jnp.concatenate(m_im[1:], axis=0)

    u = u_ref[0]
    def pad_lanes(w):
        return jnp.concatenate([w, jnp.zeros_like(w)], axis=1)

    sx_re[...] = jnp.dot(u, pad_lanes(w_st_r).astype(BF16), preferred_element_type=F32)
    sx_im[...] = jnp.dot(u, pad_lanes(w_st_i).astype(BF16), preferred_element_type=F32)

    n_batch = 8
    unroll = 8
    nj = u.shape[0] // n_batch
    alr = jnp.broadcast_to(pad_lanes(pw[sub][0]), (n_batch, LANES))
    ali = jnp.broadcast_to(pad_lanes(pw[sub][1]), (n_batch, LANES))

    def body(i, carry):
        hr, hi = carry
        for k in range(unroll):
            rows_j = pl.ds(i * unroll + k, n_batch, stride=nj)
            h_re[rows_j, :] = hr
            h_im[rows_j, :] = hi
            sr, si = sx_re[rows_j, :], sx_im[rows_j, :]
            hr, hi = alr * hr - ali * hi + sr, alr * hi + ali * hr + si
        return hr, hi

    zero = jnp.zeros((n_batch, LANES), F32)
    lax.fori_loop(0, nj // unroll, body, (zero, zero))

    y = lax.dot_general(u, tt.astype(BF16), _NT, preferred_element_type=F32)
    y += lax.dot_general(h_re[...].astype(BF16), pad_lanes(w_out_r).astype(BF16), _NT, preferred_element_type=F32)
    y -= lax.dot_general(h_im[...].astype(BF16), pad_lanes(w_out_i).astype(BF16), _NT, preferred_element_type=F32)
    y_ref[0] = y


_GROUPS_PER_TILE = LANES // SSM_GROUP_WIDTH
_SLOTS_PER_TILE = LANES // SSM_GROUP_WIDTH


def _s5_pack_kernel(x_ref, o_ref, stage_ref):
    tm = x_ref.shape[1]
    nj = tm // SSM_SUB
    n_tiles = D_MODEL // LANES
    for c in range(n_tiles):
        stage_ref[c] = x_ref[0, :, c * LANES:(c + 1) * LANES].astype(F32)
    slot = lax.broadcasted_iota(jnp.int32, (nj, LANES), 1) // SSM_GROUP_WIDTH

    def per_tile(c, carry):
        xs = [stage_ref[c, pl.ds(s, nj, stride=SSM_SUB), :] for s in range(SSM_SUB)]
        for gl in range(_GROUPS_PER_TILE):
            for h in range(SSM_SUB // _SLOTS_PER_TILE):
                acc = jnp.zeros((nj, LANES), F32)
                for sm in range(_SLOTS_PER_TILE):
                    shift = (SSM_GROUP_WIDTH * (sm - gl)) % LANES
                    acc = jnp.where(slot == sm, pltpu.roll(xs[_SLOTS_PER_TILE * h + sm], shift, 1), acc)
                o_ref[_GROUPS_PER_TILE * c + gl, :, h * LANES:(h + 1) * LANES] = acc.astype(BF16)
        return carry

    lax.fori_loop(0, n_tiles, per_tile, 0)


def _s5_unpack_kernel(y_ref, o_ref, stage_ref):
    nj = y_ref.shape[1]
    n_tiles = D_MODEL // LANES
    grp = lax.broadcasted_iota(jnp.int32, (nj, LANES), 1) // SSM_GROUP_WIDTH

    def per_tile(c, carry):
        for h in range(SSM_SUB // _SLOTS_PER_TILE):
            srcs = [y_ref[_GROUPS_PER_TILE * c + gl, :, h * LANES:(h + 1) * LANES] for gl in range(_GROUPS_PER_TILE)]
            for sm in range(_SLOTS_PER_TILE):
                acc = jnp.zeros((nj, LANES), F32)
                for gl in range(_GROUPS_PER_TILE):
                    shift = (SSM_GROUP_WIDTH * (gl - sm)) % LANES
                    acc = jnp.where(grp == gl, pltpu.roll(srcs[gl], shift, 1), acc)
                stage_ref[c, pl.ds(_SLOTS_PER_TILE * h + sm, nj, stride=SSM_SUB), :] = acc
        return carry

    lax.fori_loop(0, n_tiles, per_tile, 0)
    for c in range(n_tiles):
        o_ref[0, :, c * LANES:(c + 1) * LANES] = stage_ref[c]


def _s5(u, a_re, a_im, log_dt, b_re, b_im, c_re, c_im, tm=512):
    bsz, s_len, _ = u.shape
    assert bsz == 8, "the chunk scan holds one batch per sublane"
    g, p, gw, sub = SSM_GROUPS, SSM_STATE, SSM_GROUP_WIDTH, SSM_SUB
    nj = s_len // sub
    rows = nj * bsz
    n_t = s_len // tm
    tok_spec = pl.BlockSpec((1, tm, D_MODEL), lambda b, i: (b, i, 0))
    grp_spec = pl.BlockSpec((g, tm // sub, sub * gw), lambda b, i: (0, b * n_t + i, 0))
    stage = pltpu.VMEM((D_MODEL // LANES, tm, LANES), F32)
    ur = pl.pallas_call(
        _s5_pack_kernel,
        grid=(bsz, n_t),
        in_specs=[tok_spec],
        out_specs=grp_spec,
        out_shape=jax.ShapeDtypeStruct((g, rows, sub * gw), BF16),
        scratch_shapes=[stage],
        compiler_params=_params(("parallel", "parallel")),
        name="s5_pack",
    )(u)
    grp = lambda *shape: pl.BlockSpec((1,) + shape, lambda i: (i,) + (0,) * len(shape))
    yr = pl.pallas_call(
        _s5_kernel,
        grid=(g,),
        in_specs=[grp(1, p), grp(1, p), grp(p, 1), grp(p, 1), grp(1, 1),
                  grp(p, sub * gw), grp(p, sub * gw), grp(gw, p), grp(gw, p), grp(gw, p), grp(gw, p),
                  grp(rows, sub * gw)],
        out_specs=grp(rows, sub * gw),
        out_shape=jax.ShapeDtypeStruct((g, rows, sub * gw), F32),
        scratch_shapes=[pltpu.VMEM((rows, LANES), F32)] * 4,
        compiler_params=_params(("parallel",)),
        name="s5",
    )(a_re.reshape(g, 1, p), a_im.reshape(g, 1, p), a_re.reshape(g, p, 1), a_im.reshape(g, p, 1),
      log_dt.reshape(g, 1, 1), jnp.tile(b_re, (1, 1, sub)), jnp.tile(b_im, (1, 1, sub)),
      jnp.swapaxes(b_re, 1, 2), jnp.swapaxes(b_im, 1, 2), c_re, c_im, ur)
    return pl.pallas_call(
        _s5_unpack_kernel,
        grid=(bsz, n_t),
        in_specs=[grp_spec],
        out_specs=tok_spec,
        out_shape=jax.ShapeDtypeStruct((bsz, s_len, D_MODEL), F32),
        scratch_shapes=[stage],
        compiler_params=_params(("parallel", "parallel")),
        name="s5_unpack",
    )(yr)


def _glu_kernel(x_ref, y_ref, mod_ref, dskip_ref, w_ref, bias_ref, g_ref, b_ref, rw_ref, rb_ref,
                out_ref, cw_ref):
    mod = mod_ref[0]
    x = x_ref[0]
    u = x * (1.0 + mod[1:2]) + mod[0:1]
    y = y_ref[0] + dskip_ref[...] * u
    act = jax.nn.gelu(y).astype(BF16)
    z = jnp.dot(act, w_ref[...], preferred_element_type=F32) + bias_ref[...]
    yy = z[:, :D_MODEL] * jax.nn.sigmoid(z[:, D_MODEL:])
    out = _layernorm(DN_ALPHA * x + (1.0 + mod[2:3]) * yy, g_ref[...], b_ref[...])
    out_ref[0] = out
    cw_ref[0] = _route(out * (1.0 + mod[4:5]) + mod[3:4], rw_ref[...], rb_ref[...])


def _glu(x, y_ssm, mod, d_skip, w, bias, ln_g, ln_b, rw, rb, tm=256):
    bsz, s_len, _ = x.shape
    full = pl.BlockSpec((1, tm, D_MODEL), lambda b, i: (b, i, 0))
    vec = pl.BlockSpec((1, D_MODEL), lambda b, i: (0, 0))
    rw_pad = jnp.zeros((D_MODEL, LANES), F32).at[:, :N_EXPERTS].set(rw)
    rb_pad = jnp.zeros((1, LANES), F32).at[0, :N_EXPERTS].set(rb)
    return pl.pallas_call(
        _glu_kernel,
        grid=(bsz, s_len // tm),
        in_specs=[full, full, pl.BlockSpec((1, 6, D_MODEL), lambda b, i: (b, 0, 0)), vec,
                  pl.BlockSpec((D_MODEL, 2 * D_MODEL), lambda b, i: (0, 0)),
                  pl.BlockSpec((1, 2 * D_MODEL), lambda b, i: (0, 0)), vec, vec,
                  pl.BlockSpec((D_MODEL, LANES), lambda b, i: (0, 0)),
                  pl.BlockSpec((1, LANES), lambda b, i: (0, 0))],
        out_specs=[full, pl.BlockSpec((1, tm, LANES), lambda b, i: (b, i, 0))],
        out_shape=[jax.ShapeDtypeStruct((bsz, s_len, D_MODEL), F32),
                   jax.ShapeDtypeStruct((bsz, s_len, LANES), F32)],
        compiler_params=_params(("parallel", "parallel")),
        name="glu",
    )(x, y_ssm, mod, d_skip.reshape(1, -1), w, bias.reshape(1, -1), ln_g.reshape(1, -1), ln_b.reshape(1, -1),
      rw_pad, rb_pad)


def _route(h, w, bias):
    h_hi = h.astype(BF16)
    h_lo = (h - h_hi.astype(F32)).astype(BF16)
    w_hi = w.astype(BF16)
    w_lo = (w - w_hi.astype(F32)).astype(BF16)
    logits = (jnp.dot(h_hi, w_hi, preferred_element_type=F32)
              + jnp.dot(h_hi, w_lo, preferred_element_type=F32)
              + jnp.dot(h_lo, w_hi, preferred_element_type=F32)) + bias
    lane = lax.broadcasted_iota(jnp.int32, logits.shape, 1).astype(F32)
    logits = jnp.where(lane < N_EXPERTS, logits, NEG_INF)
    v1 = jnp.max(logits, axis=-1, keepdims=True)
    i1 = jnp.min(jnp.where(logits == v1, lane, float(LANES)), axis=-1, keepdims=True)
    rest = jnp.where(lane == i1, NEG_INF, logits)
    v2 = jnp.max(rest, axis=-1, keepdims=True)
    i2 = jnp.min(jnp.where(rest == v2, lane, float(LANES)), axis=-1, keepdims=True)
    e = jnp.exp(v2 - v1)
    g1 = 1.0 / (1.0 + e)
    g2 = e / (1.0 + e)
    return jnp.where(lane == i1, g1, 0.0) + jnp.where(lane == i2, g2, 0.0)


def _alibi_slopes():
    i = jnp.arange(N_ATTN_HEADS, dtype=F32) + 1.0
    return jnp.exp2(-8.0 * i / N_ATTN_HEADS)


def kernel(x, c, l0_ada_w, l0_ada_b, l0_w_in, l0_lam_q1, l0_lam_k1, l0_lam_q2, l0_lam_k2, l0_subln_w, l0_w_out, l0_ln1_g, l0_ln1_b, l0_ffn_w_gate, l0_ffn_w_up, l0_ffn_w_down, l0_ln2_g, l0_ln2_b, l1_ada_w, l1_ada_b, l1_a_re, l1_a_im, l1_log_dt, l1_b_re, l1_b_im, l1_c_re, l1_c_im, l1_d_skip, l1_w_glu, l1_b_glu, l1_ln1_g, l1_ln1_b, l1_router_w, l1_router_b, l1_exp_w_gate, l1_exp_w_up, l1_exp_w_down, l1_ln2_g, l1_ln2_b):
    mod0 = _ada(c, l0_ada_w, l0_ada_b)
    mod1 = _ada(c, l1_ada_w, l1_ada_b)

    q_a = jnp.full((HALF_D,), DIFF_QK_DIM ** -0.5 * LOG2E, F32)
    q_b = jnp.full((HALF_D,), HEAD_DIM ** -0.5 * LOG2E, F32)
    ones = jnp.ones((D_MODEL,), F32)
    colscale = jnp.concatenate([q_a, ones, q_b, ones]).reshape(1, -1)
    dils = tuple(dil for _, dil in DIL_PAIRS)
    proj, *planes = _qkv(x, mod0, l0_w_in.astype(BF16), colscale, dils[1:])
    slopes = _alibi_slopes()
    lam_vecs = jnp.stack([l0_lam_q1, l0_lam_k1, l0_lam_q2, l0_lam_k2]).astype(F32)
    subw = jnp.tile(l0_subln_w.astype(F32), 2).reshape(LANES, 1)
    oa = _diff_attention(proj, slopes[:N_HEADS_DIFF], lam_vecs, subw, 0)
    dil_slopes = tuple(2.0 ** (-8.0 * (h + 1.0) / N_ATTN_HEADS) for h in range(N_HEADS_DIFF, N_ATTN_HEADS))
    assert dils[0] == 1
    dil_outs = [_dilated_branch(proj[:, None], 1, dil_slopes, 3)]
    dil_outs += [_dilated_branch(pln, dil, dil_slopes, 0) for pln, dil in zip(planes, dils[1:])]
    x = _attn_out(oa, dil_outs, dils, x, mod0, l0_w_out.astype(BF16), l0_ln1_g, l0_ln1_b)
    x, u = _ffn(x, mod0, l0_ffn_w_gate.astype(BF16), l0_ffn_w_up.astype(BF16), l0_ffn_w_down.astype(BF16),
                l0_ln2_g, l0_ln2_b, next_mod=mod1)

    y_ssm = _s5(u, l1_a_re, l1_a_im, l1_log_dt, l1_b_re, l1_b_im, l1_c_re, l1_c_im)
    x, cw = _glu(x, y_ssm, mod1, l1_d_skip, l1_w_glu.astype(BF16), l1_b_glu, l1_ln1_g, l1_ln1_b,
                 l1_router_w, l1_router_b)
    x = _ffn(x, mod1, l1_exp_w_gate.astype(BF16), l1_exp_w_up.astype(BF16), l1_exp_w_down.astype(BF16),
             l1_ln2_g, l1_ln2_b, cw=cw)
    return x
```

```python
import functools
import math

import jax
import jax.numpy as jnp
from jax import lax
from jax.experimental import pallas as pl
from jax.experimental.pallas import tpu as pltpu

F32 = jnp.float32
BF16 = jnp.bfloat16

D_MODEL = 1024
DEPTH = 2
HEAD_DIM = 64
N_HEADS_DIFF = D_MODEL // (2 * HEAD_DIM)
DIFF_QK_DIM = HEAD_DIM // 2
N_HEADS_DIL = D_MODEL // (2 * HEAD_DIM)
N_ATTN_HEADS = N_HEADS_DIFF + N_HEADS_DIL
DIL_PAIRS = ((128, 1), (512, 4), (2048, 16))
ATTN_BLOCK = 128
SSM_GROUP_WIDTH = 16
SSM_GROUPS = D_MODEL // SSM_GROUP_WIDTH
SSM_STATE = 64
SSM_SUB = 16
D_FF = 2816
N_EXPERTS = 8
D_FF_EXPERT = 1408
DN_ALPHA = (2.0 * DEPTH) ** 0.25
EPS = 1e-5
LOG2E = 1.4426950408889634
LANES = 128
HALF_D = D_MODEL // 2
NEG_INF = float("-inf")

_NT = (((1,), (1,)), ((), ()))
_HI = lax.Precision.HIGHEST


def _params(sem, vmem_mb=48):
    return pltpu.CompilerParams(dimension_semantics=sem, vmem_limit_bytes=vmem_mb * 1024 * 1024)


def _layernorm(z, g, b):
    mu = jnp.mean(z, axis=-1, keepdims=True)
    zc = z - mu
    var = jnp.mean(zc * zc, axis=-1, keepdims=True)
    return zc * lax.rsqrt(var + EPS) * g + b


def _ada_kernel(c_ref, w_ref, b_ref, o_ref):
    h = jax.nn.silu(c_ref[...])
    o_ref[...] = jnp.dot(h.astype(BF16), w_ref[...].astype(BF16),
                         preferred_element_type=F32) + b_ref[...]


def _ada(c, w, b):
    bsz = c.shape[0]
    n = w.shape[1]
    tn = 1024
    out = pl.pallas_call(
        _ada_kernel,
        grid=(n // tn,),
        in_specs=[pl.BlockSpec((bsz, D_MODEL), lambda j: (0, 0)),
                  pl.BlockSpec((D_MODEL, tn), lambda j: (0, j)),
                  pl.BlockSpec((1, tn), lambda j: (0, j))],
        out_specs=pl.BlockSpec((bsz, tn), lambda j: (0, j)),
        out_shape=jax.ShapeDtypeStruct((bsz, n), F32),
        compiler_params=_params(("arbitrary",)),
        name="ada",
    )(c, w, b.reshape(1, n))
    return out.reshape(bsz, 6, D_MODEL)


def _qkv_kernel(x_ref, mod_ref, w_ref, cs_ref, o_ref, *rest, dils):
    perm_refs, stage_ref = rest[:-1], rest[-1]
    tm = x_ref.shape[1]
    mod = mod_ref[0]
    h = (x_ref[0] * (1.0 + mod[1:2]) + mod[0:1]).astype(BF16)
    for j in range(6):
        sl = slice(j * HALF_D, (j + 1) * HALF_D)
        acc = jnp.dot(h, w_ref[:, sl], preferred_element_type=F32) * cs_ref[:, sl]
        o_ref[0, :, sl] = acc.astype(BF16)
        if j >= 3 and dils:
            for c in range(HALF_D // LANES):
                stage_ref[(j - 3) * (HALF_D // LANES) + c] = acc[:, c * LANES:(c + 1) * LANES]
    for dil, p_ref in zip(dils, perm_refs):
        for r in range(dil):
            for c in range(stage_ref.shape[0]):
                rows = stage_ref[c, pl.ds(r, tm // dil, stride=dil), :]
                p_ref[0, r, :, c * LANES:(c + 1) * LANES] = rows.astype(BF16)


def _qkv(x, mod, w, colscale, dils, tm=512):
    bsz, s_len, _ = x.shape
    n = w.shape[1]
    nb = n // 2
    out_specs = [pl.BlockSpec((1, tm, n), lambda b, i: (b, i, 0))]
    out_shape = [jax.ShapeDtypeStruct((bsz, s_len, n), BF16)]
    for dil in dils:
        out_specs.append(pl.BlockSpec((1, dil, tm // dil, nb), lambda b, i: (b, 0, i, 0)))
        out_shape.append(jax.ShapeDtypeStruct((bsz, dil, s_len // dil, nb), BF16))
    return pl.pallas_call(
        functools.partial(_qkv_kernel, dils=dils),
        grid=(bsz, s_len // tm),
        in_specs=[pl.BlockSpec((1, tm, D_MODEL), lambda b, i: (b, i, 0)),
                  pl.BlockSpec((1, 6, D_MODEL), lambda b, i: (b, 0, 0)),
                  pl.BlockSpec((D_MODEL, n), lambda b, i: (0, 0)),
                  pl.BlockSpec((1, n), lambda b, i: (0, 0))],
        out_specs=out_specs,
        out_shape=out_shape,
        scratch_shapes=[pltpu.VMEM((nb // LANES, tm, LANES), F32)],
        compiler_params=_params(("parallel", "parallel")),
        name="qkv",
    )(x, mod, w, colscale)


def _diff_kernel(slope_ref, lam_ref, subw_ref, q_ref, k_ref, v_ref, o_ref,
                 acc_ref, m_ref, mask_ref, vat_ref, sa_ref, sb_ref, *, tk, lam_init):
    tq = 2 * tk
    pr = pl.program_id(1)
    qi = pl.program_id(2)
    n_kb = k_ref.shape[1] // tk
    sl2 = [slope_ref[2 * pr + hh] * LOG2E for hh in range(2)]

    @pl.when(qi == 0)
    def _():
        kr = lax.broadcasted_iota(jnp.int32, (tk, tq), 0)
        qc = lax.broadcasted_iota(jnp.int32, (tk, tq), 1)
        mask_ref[0] = jnp.where(qc >= kr, 0.0, NEG_INF)
        mask_ref[1] = jnp.where(qc - tk >= kr, 0.0, NEG_INF)
        row = lax.broadcasted_iota(jnp.int32, (LANES, tk), 0)
        for j in range(n_kb):
            vt = v_ref[0, j * tk:(j + 1) * tk, :].astype(F32).T
            vat_ref[0, j] = jnp.where(row < HEAD_DIM, vt, 1.0).astype(BF16)
            vat_ref[1, j] = jnp.where(row >= HEAD_DIM, vt, 1.0).astype(BF16)

    lane = lax.broadcasted_iota(jnp.int32, (1, LANES), 1)
    q = q_ref[0]
    feats = []
    for hh in range(2):
        slv = jnp.zeros((1, LANES), F32) + sl2[hh]
        hi = slv.astype(BF16).astype(F32)
        r1 = slv - hi
        lo = r1.astype(BF16).astype(F32)
        lo2 = (r1 - lo).astype(BF16).astype(F32)
        feat = jnp.where(lane == 0, hi, jnp.where(lane == 1, lo, jnp.where(lane == 2, lo2, 0.0)))
        feats.append(jnp.broadcast_to(feat, (tq, LANES)).astype(BF16))
    q_aug = [jnp.concatenate([jnp.where(lane // DIFF_QK_DIM == m, q, jnp.zeros_like(q)), feats[m // 2]], axis=1)
             for m in range(4)]
    key_pos = lax.broadcasted_iota(jnp.int32, (tk, LANES), 0).astype(F32)
    pos_feat = jnp.where(lane < 3, key_pos, 0.0).astype(BF16)
    acc_ref[...] = jnp.zeros_like(acc_ref)
    m_ref[...] = jnp.full_like(m_ref, NEG_INF)

    def scores(j, s_ref):
        off = pl.multiple_of(j * tk, tk)
        k_aug = jnp.concatenate([k_ref[0, pl.ds(off, tk), :], pos_feat], axis=1)
        for m in range(4):
            s_ref[m] = lax.dot_general(k_aug, q_aug[m], _NT, preferred_element_type=F32)

    def softmax_pv(j, s_ref, variant):
        blk_off = (qi * tq - j * tk).astype(F32)
        ps, alphas = [], []
        for m in range(4):
            t = s_ref[m] if variant == 0 else s_ref[m] + mask_ref[variant - 1]
            c = -sl2[m // 2] * blk_off
            m_old = m_ref[m]
            m_new = jnp.maximum(m_old, jnp.max(t, axis=0, keepdims=True) + c)
            alphas.append(jnp.exp2(m_old - m_new))
            ps.append(jnp.exp2(t - (m_new - c)).astype(BF16))
            m_ref[m] = m_new
        for m in range(4):
            acc_ref[m] = alphas[m] * acc_ref[m] + jnp.dot(vat_ref[m // 2, j], ps[m], preferred_element_type=F32)

    scores(0, sa_ref)

    def body(i, carry):
        j = 2 * i
        scores(j + 1, sb_ref)
        softmax_pv(j, sa_ref, 0)
        scores(j + 2, sa_ref)
        softmax_pv(j + 1, sb_ref, 0)
        return carry

    lax.fori_loop(0, qi, body, 0)
    scores(2 * qi + 1, sb_ref)
    softmax_pv(2 * qi, sa_ref, 1)
    softmax_pv(2 * qi + 1, sb_ref, 2)

    lam_v = lam_ref[...]
    lam = (jnp.exp(jnp.sum(lam_v[0:1] * lam_v[1:2], axis=-1, keepdims=True))
           - jnp.exp(jnp.sum(lam_v[2:3] * lam_v[3:4], axis=-1, keepdims=True)) + lam_init)
    normed = []
    for hh in range(2):
        o_rows = slice(hh * HEAD_DIM, (hh + 1) * HEAD_DIM)
        l_row = slice((1 - hh) * HEAD_DIM, (1 - hh) * HEAD_DIM + 1)
        outs = []
        for mm in range(2):
            a = acc_ref[2 * hh + mm]
            outs.append(a[o_rows] / a[l_row])
        d = outs[0] - lam * outs[1]
        ms = jnp.mean(d * d, axis=0, keepdims=True)
        normed.append(d * lax.rsqrt(ms + EPS))
    y_t = jnp.concatenate(normed, axis=0) * subw_ref[...] * (1.0 - lam_init)
    o_ref[0] = y_t.T.astype(BF16)


def _diff_attention(proj, slopes, lam_vecs, subw, layer_idx, tk=256):
    bsz, s_len, _ = proj.shape
    tq = 2 * tk
    n_pairs = N_HEADS_DIFF // 2
    lam_init = 0.8 - 0.6 * math.exp(-0.3 * layer_idx)
    kern = functools.partial(_diff_kernel, tk=tk, lam_init=lam_init)
    return pl.pallas_call(
        kern,
        grid=(bsz, n_pairs, s_len // tq),
        in_specs=[pl.BlockSpec(memory_space=pltpu.SMEM),
                  pl.BlockSpec((4, DIFF_QK_DIM), lambda b, p, i: (0, 0)),
                  pl.BlockSpec((LANES, 1), lambda b, p, i: (0, 0)),
                  pl.BlockSpec((1, tq, LANES), lambda b, p, i: (b, i, p)),
                  pl.BlockSpec((1, s_len, LANES), lambda b, p, i: (b, 0, n_pairs + p)),
                  pl.BlockSpec((1, s_len, LANES), lambda b, p, i: (b, 0, 2 * n_pairs + p))],
        out_specs=pl.BlockSpec((1, tq, LANES), lambda b, p, i: (b, i, p)),
        out_shape=jax.ShapeDtypeStruct((bsz, s_len, HALF_D), BF16),
        scratch_shapes=[pltpu.VMEM((4, LANES, tq), F32),
                        pltpu.VMEM((4, 1, tq), F32),
                        pltpu.VMEM((2, tk, tq), F32),
                        pltpu.VMEM((2, s_len // tk, LANES, tk), BF16),
                        pltpu.VMEM((4, tk, tq), F32),
                        pltpu.VMEM((4, tk, tq), F32)],
        compiler_params=_params(("parallel", "parallel", "arbitrary")),
        name="diff_attn",
    )(slopes, lam_vecs, subw, proj, proj, proj)


def _dil_kernel(q_ref, kp_ref, kc_ref, vp_ref, vc_ref, o_ref, l_ref, bias_ref, *, dil, slopes):
    blk = ATTN_BLOCK
    n = pl.program_id(2)
    first = (pl.program_id(0) == 0) & (pl.program_id(1) == 0) & (n == 0)

    @pl.when(first)
    def _():
        qi = lax.broadcasted_iota(jnp.int32, (blk, 2 * blk), 0)
        kj = lax.broadcasted_iota(jnp.int32, (blk, 2 * blk), 1)
        dist = qi + blk - kj
        valid = (dist >= 0) & (dist <= blk)
        dist_f = (dist * dil).astype(F32)
        for h in range(N_HEADS_DIL):
            bias = (-slopes[h] * LOG2E) * dist_f
            bias_ref[h, 0] = jnp.where(valid, bias, NEG_INF)
            bias_ref[h, 1] = jnp.where(valid & (kj >= blk), bias, NEG_INF)

    first_sel = (n == 0).astype(jnp.int32)
    lane = lax.broadcasted_iota(jnp.int32, (1, LANES), 1)
    even = lane < HEAD_DIM
    one = jnp.ones((), BF16)
    for i in range(q_ref.shape[0] // blk):
        rows = slice(i * blk, (i + 1) * blk)
        before = slice((i - 1) * blk, i * blk)
        sel = first_sel if i == 0 else 0
        for p in range(N_HEADS_DIL // 2):
            sl = slice(p * LANES, (p + 1) * LANES)
            qp = q_ref[rows, sl]
            k_prev = kp_ref[:, sl] if i == 0 else kc_ref[before, sl]
            v_prev = vp_ref[:, sl] if i == 0 else vc_ref[before, sl]
            kcat = jnp.concatenate([k_prev, kc_ref[rows, sl]], axis=0)
            vcat = jnp.concatenate([v_prev, vc_ref[rows, sl]], axis=0)
            accs, ms = [], []
            for hh in range(2):
                keep = even if hh == 0 else jnp.logical_not(even)
                qh = jnp.where(keep, qp, jnp.zeros_like(qp))
                va = jnp.where(keep, vcat, one)
                s = lax.dot_general(qh, kcat, _NT, preferred_element_type=F32)
                t = s + bias_ref[2 * p + hh, sel]
                m = jnp.max(t, axis=-1, keepdims=True)
                pm = jnp.exp2(t - m)
                accs.append(jnp.dot(pm.astype(BF16), va, preferred_element_type=F32))
                ms.append(m)
            o_un = jnp.where(even, accs[0], accs[1])
            l_sum = pltpu.roll(jnp.where(even, accs[1], accs[0]), HEAD_DIM, 1)
            m_pair = jnp.where(even, ms[0], ms[1])
            o_ref[rows, sl] = o_un / l_sum
            l_ref[rows, sl] = m_pair + jnp.log2(l_sum)


_DIL_BLOCKS_PER_STEP = 8


def _dilated_branch(qkv_planes, dil, slopes, col0):
    bsz, _, sub_len, _ = qkv_planes.shape
    blk = ATTN_BLOCK
    nb = sub_len // blk
    qb = min(_DIL_BLOCKS_PER_STEP, nb)
    kern = functools.partial(_dil_kernel, dil=dil, slopes=slopes)

    def cur(col):
        return pl.BlockSpec((None, None, qb * blk, HALF_D), lambda b, r, n: (b, r, n, col0 + col))

    def prev(col):
        return pl.BlockSpec((None, None, blk, HALF_D),
                            lambda b, r, n: (b, r, jnp.maximum(n * qb - 1, 0), col0 + col))

    out_spec = pl.BlockSpec((None, None, qb * blk, HALF_D), lambda b, r, n: (b, r, n, 0))
    return pl.pallas_call(
        kern,
        grid=(bsz, dil, nb // qb),
        in_specs=[cur(0), prev(1), cur(1), prev(2), cur(2)],
        out_specs=[out_spec, out_spec],
        out_shape=[jax.ShapeDtypeStruct((bsz, dil, sub_len, HALF_D), F32)] * 2,
        scratch_shapes=[pltpu.VMEM((N_HEADS_DIL, 2, blk, 2 * blk), F32)],
        compiler_params=_params(("arbitrary", "arbitrary", "arbitrary")),
        name=f"dilated_d{dil}",
    )(*([qkv_planes] * 5))


def _attn_out_kernel(oa_ref, o1_ref, l1_ref, o2_ref, l2_ref, o3_ref, l3_ref, x_ref, mod_ref,
                     w_ref, g_ref, b_ref, out_ref, tok_ref, ob_ref, *, dils):
    tm = x_ref.shape[1]
    branch_refs = ((o1_ref, l1_ref), (o2_ref, l2_ref), (o3_ref, l3_ref))
    for c in range(HALF_D // LANES):
        cols = slice(c * LANES, (c + 1) * LANES)
        vals = []
        for g, (dil, pair) in enumerate(zip(dils, branch_refs)):
            for a, src_ref in enumerate(pair):
                if dil == 1:
                    vals.append(src_ref[0, 0, :, cols])
                else:
                    for r in range(dil):
                        tok_ref[g, a, pl.ds(r, tm // dil, stride=dil), :] = src_ref[0, r, :, cols]
                    vals.append(tok_ref[g, a])
        o1, l1, o2, l2, o3, l3 = vals
        mx = jnp.maximum(jnp.maximum(l1, l2), l3)
        e1, e2, e3 = jnp.exp2(l1 - mx), jnp.exp2(l2 - mx), jnp.exp2(l3 - mx)
        ob_ref[:, cols] = ((e1 * o1 + e2 * o2 + e3 * o3) / (e1 + e2 + e3)).astype(BF16)
    y = (jnp.dot(oa_ref[0], w_ref[:HALF_D], preferred_element_type=F32)
         + jnp.dot(ob_ref[...], w_ref[HALF_D:], preferred_element_type=F32))
    mod = mod_ref[0]
    z = DN_ALPHA * x_ref[0] + (1.0 + mod[2:3]) * y
    out_ref[0] = _layernorm(z, g_ref[...], b_ref[...])


def _attn_out(oa, dil_outs, dils, x, mod, w, ln_g, ln_b, tm=512):
    bsz, s_len, _ = x.shape
    half = pl.BlockSpec((1, tm, HALF_D), lambda b, i: (b, i, 0))
    full = pl.BlockSpec((1, tm, D_MODEL), lambda b, i: (b, i, 0))
    vec = pl.BlockSpec((1, D_MODEL), lambda b, i: (0, 0))
    plane_specs = []
    for dil in dils:
        spec = pl.BlockSpec((1, dil, tm // dil, HALF_D), lambda b, i: (b, 0, i, 0))
        plane_specs += [spec, spec]
    flat = [a for pair in dil_outs for a in pair]
    return pl.pallas_call(
        functools.partial(_attn_out_kernel, dils=dils),
        grid=(bsz, s_len // tm),
        in_specs=[half] + plane_specs + [full, pl.BlockSpec((1, 6, D_MODEL), lambda b, i: (b, 0, 0)),
                                         pl.BlockSpec((D_MODEL, D_MODEL), lambda b, i: (0, 0)), vec, vec],
        out_specs=full,
        out_shape=jax.ShapeDtypeStruct((bsz, s_len, D_MODEL), F32),
        scratch_shapes=[pltpu.VMEM((len(dils), 2, tm, LANES), F32), pltpu.VMEM((tm, HALF_D), BF16)],
        compiler_params=_params(("parallel", "parallel")),
        name="attn_out",
    )(oa, *flat, x, mod, w, ln_g.reshape(1, -1), ln_b.reshape(1, -1))


def _ffn_kernel(*refs, gated, emit_next):
    it = iter(refs)
    x_ref, mod_ref = next(it), next(it)
    cw_ref = next(it) if gated else None
    wg_ref, wu_ref, wd_ref, g_ref, b_ref = next(it), next(it), next(it), next(it), next(it)
    nmod_ref = next(it) if emit_next else None
    out_ref = next(it)
    nxt_ref = next(it) if emit_next else None
    h_ref, acc_ref = next(it), next(it)

    j = pl.program_id(2)
    mod = mod_ref[0]

    @pl.when(j == 0)
    def _():
        h_ref[...] = (x_ref[0] * (1.0 + mod[4:5]) + mod[3:4]).astype(BF16)
        acc_ref[...] = jnp.zeros_like(acc_ref)

    h = h_ref[...]
    wg = wg_ref.at[0] if gated else wg_ref
    wu = wu_ref.at[0] if gated else wu_ref
    wd = wd_ref.at[0] if gated else wd_ref
    width = wg.shape[1]
    y = None
    for lo in range(0, width, _FF_CHUNK):
        hi = min(lo + _FF_CHUNK, width)
        gate = jnp.dot(h, wg[:, lo:hi], preferred_element_type=F32)
        up = jnp.dot(h, wu[:, lo:hi], preferred_element_type=F32)
        act = (jax.nn.silu(gate) * up).astype(BF16)
        part = jnp.dot(act, wd[lo:hi, :], preferred_element_type=F32)
        y = part if y is None else y + part
    if gated:
        lane = lax.broadcasted_iota(jnp.int32, (1, LANES), 1)
        cw = jnp.sum(jnp.where(lane == j, cw_ref[0], 0.0), axis=-1, keepdims=True)
        y = cw * y
    acc_ref[...] += y

    @pl.when(j == pl.num_programs(2) - 1)
    def _():
        z = DN_ALPHA * x_ref[0] + (1.0 + mod[5:6]) * acc_ref[...]
        out = _layernorm(z, g_ref[...], b_ref[...])
        out_ref[0] = out
        if emit_next:
            nmod = nmod_ref[0]
            nxt_ref[0] = (out * (1.0 + nmod[1:2]) + nmod[0:1]).astype(BF16)


_FF_CHUNK = 768


def _ffn(x, mod, wg, wu, wd, ln_g, ln_b, *, cw=None, next_mod=None, tm=1024, tf=1408):
    bsz, s_len, _ = x.shape
    gated = cw is not None
    emit_next = next_mod is not None
    full = pl.BlockSpec((1, tm, D_MODEL), lambda b, i, j: (b, i, 0))
    modspec = pl.BlockSpec((1, 6, D_MODEL), lambda b, i, j: (b, 0, 0))
    vec = pl.BlockSpec((1, D_MODEL), lambda b, i, j: (0, 0))
    if gated:
        n_inner = wg.shape[0]
        w_in_spec = pl.BlockSpec((1, D_MODEL, wg.shape[2]), lambda b, i, j: (j, 0, 0))
        w_dn_spec = pl.BlockSpec((1, wd.shape[1], D_MODEL), lambda b, i, j: (j, 0, 0))
    else:
        n_inner = wg.shape[1] // tf
        w_in_spec = pl.BlockSpec((D_MODEL, tf), lambda b, i, j: (0, j))
        w_dn_spec = pl.BlockSpec((tf, D_MODEL), lambda b, i, j: (j, 0))
    in_specs = [full, modspec]
    args = [x, mod]
    if gated:
        in_specs.append(pl.BlockSpec((1, tm, LANES), lambda b, i, j: (b, i, 0)))
        args.append(cw)
    in_specs += [w_in_spec, w_in_spec, w_dn_spec, vec, vec]
    args += [wg, wu, wd, ln_g.reshape(1, -1), ln_b.reshape(1, -1)]
    out_specs = [full]
    out_shape = [jax.ShapeDtypeStruct((bsz, s_len, D_MODEL), F32)]
    if emit_next:
        in_specs.append(modspec)
        args.append(next_mod)
        out_specs.append(full)
        out_shape.append(jax.ShapeDtypeStruct((bsz, s_len, D_MODEL), BF16))
    res = pl.pallas_call(
        functools.partial(_ffn_kernel, gated=gated, emit_next=emit_next),
        grid=(bsz, s_len // tm, n_inner),
        in_specs=in_specs,
        out_specs=out_specs,
        out_shape=out_shape,
        scratch_shapes=[pltpu.VMEM((tm, D_MODEL), BF16), pltpu.VMEM((tm, D_MODEL), F32)],
        compiler_params=_params(("parallel", "parallel", "arbitrary"), vmem_mb=56),
        name="moe_ffn" if gated else "ffn",
    )(*args)
    return res if emit_next else res[0]


def _s5_kernel(ar_r_ref, ai_r_ref, ar_c_ref, ai_c_ref, ldt_ref, brep_re_ref, brep_im_ref,
               bt_re_ref, bt_im_ref, c_re_ref, c_im_ref, u_ref, y_ref, sx_re, sx_im, h_re, h_im):
    sub = SSM_SUB
    gw = SSM_GROUP_WIDTH
    n_lane = sub * gw
    dt = jnp.exp(ldt_ref[0])

    def discretise(ar, ai):
        mag = jnp.exp(dt * ar)
        abr, abi = mag * jnp.cos(dt * ai), mag * jnp.sin(dt * ai)
        den = ar * ar + ai * ai
        nr, ni = abr - 1.0, abi
        return abr, abi, (nr * ar + ni * ai) / den, (ni * ar - nr * ai) / den

    abr, abi, zr, zi = discretise(ar_r_ref[0], ai_r_ref[0])
    _, _, zrc, zic = discretise(ar_c_ref[0], ai_c_ref[0])
    bt_r, bt_i = bt_re_ref[0], bt_im_ref[0]
    bbt_r, bbt_i = zr * bt_r - zi * bt_i, zr * bt_i + zi * bt_r
    brep_r, brep_i = brep_re_ref[0], brep_im_ref[0]
    bbrep_r, bbrep_i = zrc * brep_r - zic * brep_i, zrc * brep_i + zic * brep_r

    pw = []
    pr, pi = jnp.ones_like(abr), jnp.zeros_like(abr)
    for _ in range(sub + 1):
        pw.append((pr, pi))
        pr, pi = pr * abr - pi * abi, pr * abi + pi * abr
    cr, ci = c_re_ref[0], c_im_ref[0]
    m_re = [cr * a - ci * b for a, b in pw]
    m_im = [cr * b + ci * a for a, b in pw]

    kk = (jnp.dot(jnp.concatenate(m_re[:sub], axis=0), bbrep_r, precision=_HI, preferred_element_type=F32)
          - jnp.dot(jnp.concatenate(m_im[:sub], axis=0), bbrep_i, precision=_HI, preferred_element_type=F32))
    lane_slot = lax.broadcasted_iota(jnp.int32, (n_lane, n_lane), 1) // gw
    tt = jnp.zeros((n_lane, n_lane), F32)
    for s in range(sub):
        if s == 0:
            shifted = kk
        else:
            shifted = jnp.concatenate([jnp.zeros((s * gw, n_lane), F32), kk[:n_lane - s * gw]], axis=0)
        tt = jnp.where(lane_slot == s, shifted, tt)

    w_st_r = jnp.concatenate([bbt_r * pw[sub - 1 - s][0] - bbt_i * pw[sub - 1 - s][1] for s in range(sub)], axis=0)
    w_st_i = jnp.concatenate([bbt_r * pw[sub - 1 - s][1] + bbt_i * pw[sub - 1 - s][0] for s in range(sub)], axis=0)
    w_out_r = jnp.concatenate(m_re[1:], axis=0)
    w_out_i = jnp.concatenate(m_im[1:], axis=0)

    u = u_ref[0]
    def pad_lanes(w):
        return jnp.concatenate([w, jnp.zeros_like(w)], axis=1)

    sx_re[...] = jnp.dot(u, pad_lanes(w_st_r).astype(BF16), preferred_element_type=F32)
    sx_im[...] = jnp.dot(u, pad_lanes(w_st_i).astype(BF16), preferred_element_type=F32)

    n_batch = 8
    unroll = 8
    nj = u.shape[0] // n_batch
    alr = jnp.broadcast_to(pad_lanes(pw[sub][0]), (n_batch, LANES))
    ali = jnp.broadcast_to(pad_lanes(pw[sub][1]), (n_batch, LANES))

    def body(i, carry):
        hr, hi = carry
        for k in range(unroll):
            rows_j = pl.ds(i * unroll + k, n_batch, stride=nj)
            h_re[rows_j, :] = hr
            h_im[rows_j, :] = hi
            sr, si = sx_re[rows_j, :], sx_im[rows_j, :]
            hr, hi = alr * hr - ali * hi + sr, alr * hi + ali * hr + si
        return hr, hi

    zero = jnp.zeros((n_batch, LANES), F32)
    lax.fori_loop(0, nj // unroll, body, (zero, zero))

    y = lax.dot_general(u, tt.astype(BF16), _NT, preferred_element_type=F32)
    y += lax.dot_general(h_re[...].astype(BF16), pad_lanes(w_out_r).astype(BF16), _NT, preferred_element_type=F32)
    y -= lax.dot_general(h_im[...].astype(BF16), pad_lanes(w_out_i).astype(BF16), _NT, preferred_element_type=F32)
    y_ref[0] = y


_GROUPS_PER_TILE = LANES // SSM_GROUP_WIDTH
_SLOTS_PER_TILE = LANES // SSM_GROUP_WIDTH


def _s5_pack_kernel(x_ref, o_ref, stage_ref):
    tm = x_ref.shape[1]
    nj = tm // SSM_SUB
    n_tiles = D_MODEL // LANES
    for c in range(n_tiles):
        stage_ref[c] = x_ref[0, :, c * LANES:(c + 1) * LANES].astype(F32)
    slot = lax.broadcasted_iota(jnp.int32, (nj, LANES), 1) // SSM_GROUP_WIDTH

    def per_tile(c, carry):
        xs = [stage_ref[c, pl.ds(s, nj, stride=SSM_SUB), :] for s in range(SSM_SUB)]
        for gl in range(_GROUPS_PER_TILE):
            for h in range(SSM_SUB // _SLOTS_PER_TILE):
                acc = jnp.zeros((nj, LANES), F32)
                for sm in range(_SLOTS_PER_TILE):
                    shift = (SSM_GROUP_WIDTH * (sm - gl)) % LANES
                    acc = jnp.where(slot == sm, pltpu.roll(xs[_SLOTS_PER_TILE * h + sm], shift, 1), acc)
                o_ref[_GROUPS_PER_TILE * c + gl, :, h * LANES:(h + 1) * LANES] = acc.astype(BF16)
        return carry

    lax.fori_loop(0, n_tiles, per_tile, 0)


def _s5_unpack_kernel(y_ref, o_ref, stage_ref):
    nj = y_ref.shape[1]
    n_tiles = D_MODEL // LANES
    grp = lax.broadcasted_iota(jnp.int32, (nj, LANES), 1) // SSM_GROUP_WIDTH

    def per_tile(c, carry):
        for h in range(SSM_SUB // _SLOTS_PER_TILE):
            srcs = [y_ref[_GROUPS_PER_TILE * c + gl, :, h * LANES:(h + 1) * LANES] for gl in range(_GROUPS_PER_TILE)]
            for sm in range(_SLOTS_PER_TILE):
                acc = jnp.zeros((nj, LANES), F32)
                for gl in range(_GROUPS_PER_TILE):
                    shift = (SSM_GROUP_WIDTH * (gl - sm)) % LANES
                    acc = jnp.where(grp == gl, pltpu.roll(srcs[gl], shift, 1), acc)
                stage_ref[c, pl.ds(_SLOTS_PER_TILE * h + sm, nj, stride=SSM_SUB), :] = acc
        return carry

    lax.fori_loop(0, n_tiles, per_tile, 0)
    for c in range(n_tiles):
        o_ref[0, :, c * LANES:(c + 1) * LANES] = stage_ref[c]


def _s5(u, a_re, a_im, log_dt, b_re, b_im, c_re, c_im, tm=512):
    bsz, s_len, _ = u.shape
    assert bsz == 8, "the chunk scan holds one batch per sublane"
    g, p, gw, sub = SSM_GROUPS, SSM_STATE, SSM_GROUP_WIDTH, SSM_SUB
    nj = s_len // sub
    rows = nj * bsz
    n_t = s_len // tm
    tok_spec = pl.BlockSpec((1, tm, D_MODEL), lambda b, i: (b, i, 0))
    grp_spec = pl.BlockSpec((g, tm // sub, sub * gw), lambda b, i: (0, b * n_t + i, 0))
    stage = pltpu.VMEM((D_MODEL // LANES, tm, LANES), F32)
    ur = pl.pallas_call(
        _s5_pack_kernel,
        grid=(bsz, n_t),
        in_specs=[tok_spec],
        out_specs=grp_spec,
        out_shape=jax.ShapeDtypeStruct((g, rows, sub * gw), BF16),
        scratch_shapes=[stage],
        compiler_params=_params(("parallel", "parallel")),
        name="s5_pack",
    )(u)
    grp = lambda *shape: pl.BlockSpec((1,) + shape, lambda i: (i,) + (0,) * len(shape))
    yr = pl.pallas_call(
        _s5_kernel,
        grid=(g,),
        in_specs=[grp(1, p), grp(1, p), grp(p, 1), grp(p, 1), grp(1, 1),
                  grp(p, sub * gw), grp(p, sub * gw), grp(gw, p), grp(gw, p), grp(gw, p), grp(gw, p),
                  grp(rows, sub * gw)],
        out_specs=grp(rows, sub * gw),
        out_shape=jax.ShapeDtypeStruct((g, rows, sub * gw), F32),
        scratch_shapes=[pltpu.VMEM((rows, LANES), F32)] * 4,
        compiler_params=_params(("parallel",)),
        name="s5",
    )(a_re.reshape(g, 1, p), a_im.reshape(g, 1, p), a_re.reshape(g, p, 1), a_im.reshape(g, p, 1),
      log_dt.reshape(g, 1, 1), jnp.tile(b_re, (1, 1, sub)), jnp.tile(b_im, (1, 1, sub)),
      jnp.swapaxes(b_re, 1, 2), jnp.swapaxes(b_im, 1, 2), c_re, c_im, ur)
    return pl.pallas_call(
        _s5_unpack_kernel,
        grid=(bsz, n_t),
        in_specs=[grp_spec],
        out_specs=tok_spec,
        out_shape=jax.ShapeDtypeStruct((bsz, s_len, D_MODEL), F32),
        scratch_shapes=[stage],
        compiler_params=_params(("parallel", "parallel")),
        name="s5_unpack",
    )(yr)


def _glu_kernel(x_ref, y_ref, mod_ref, dskip_ref, w_ref, bias_ref, g_ref, b_ref, rw_ref, rb_ref,
                out_ref, cw_ref):
    mod = mod_ref[0]
    x = x_ref[0]
    u = x * (1.0 + mod[1:2]) + mod[0:1]
    y = y_ref[0] + dskip_ref[...] * u
    act = jax.nn.gelu(y).astype(BF16)
    z = jnp.dot(act, w_ref[...], preferred_element_type=F32) + bias_ref[...]
    yy = z[:, :D_MODEL] * jax.nn.sigmoid(z[:, D_MODEL:])
    out = _layernorm(DN_ALPHA * x + (1.0 + mod[2:3]) * yy, g_ref[...], b_ref[...])
    out_ref[0] = out
    cw_ref[0] = _route(out * (1.0 + mod[4:5]) + mod[3:4], rw_ref[...], rb_ref[...])


def _glu(x, y_ssm, mod, d_skip, w, bias, ln_g, ln_b, rw, rb, tm=512):
    bsz, s_len, _ = x.shape
    full = pl.BlockSpec((1, tm, D_MODEL), lambda b, i: (b, i, 0))
    vec = pl.BlockSpec((1, D_MODEL), lambda b, i: (0, 0))
    rw_pad = jnp.zeros((D_MODEL, LANES), F32).at[:, :N_EXPERTS].set(rw)
    rb_pad = jnp.zeros((1, LANES), F32).at[0, :N_EXPERTS].set(rb)
    return pl.pallas_call(
        _glu_kernel,
        grid=(bsz, s_len // tm),
        in_specs=[full, full, pl.BlockSpec((1, 6, D_MODEL), lambda b, i: (b, 0, 0)), vec,
                  pl.BlockSpec((D_MODEL, 2 * D_MODEL), lambda b, i: (0, 0)),
                  pl.BlockSpec((1, 2 * D_MODEL), lambda b, i: (0, 0)), vec, vec,
                  pl.BlockSpec((D_MODEL, LANES), lambda b, i: (0, 0)),
                  pl.BlockSpec((1, LANES), lambda b, i: (0, 0))],
        out_specs=[full, pl.BlockSpec((1, tm, LANES), lambda b, i: (b, i, 0))],
        out_shape=[jax.ShapeDtypeStruct((bsz, s_len, D_MODEL), F32),
                   jax.ShapeDtypeStruct((bsz, s_len, LANES), F32)],
        compiler_params=_params(("parallel", "parallel")),
        name="glu",
    )(x, y_ssm, mod, d_skip.reshape(1, -1), w, bias.reshape(1, -1), ln_g.reshape(1, -1), ln_b.reshape(1, -1),
      rw_pad, rb_pad)


def _route(h, w, bias):
    h_hi = h.astype(BF16)
    h_lo = (h - h_hi.astype(F32)).astype(BF16)
    w_hi = w.astype(BF16)
    w_lo = (w - w_hi.astype(F32)).astype(BF16)
    logits = (jnp.dot(h_hi, w_hi, preferred_element_type=F32)
              + jnp.dot(h_hi, w_lo, preferred_element_type=F32)
              + jnp.dot(h_lo, w_hi, preferred_element_type=F32)) + bias
    lane = lax.broadcasted_iota(jnp.int32, logits.shape, 1).astype(F32)
    logits = jnp.where(lane < N_EXPERTS, logits, NEG_INF)
    v1 = jnp.max(logits, axis=-1, keepdims=True)
    i1 = jnp.min(jnp.where(logits == v1, lane, float(LANES)), axis=-1, keepdims=True)
    rest = jnp.where(lane == i1, NEG_INF, logits)
    v2 = jnp.max(rest, axis=-1, keepdims=True)
    i2 = jnp.min(jnp.where(rest == v2, lane, float(LANES)), axis=-1, keepdims=True)
    e = jnp.exp(v2 - v1)
    g1 = 1.0 / (1.0 + e)
    g2 = e / (1.0 + e)
    return jnp.where(lane == i1, g1, 0.0) + jnp.where(lane == i2, g2, 0.0)


def _alibi_slopes():
    i = jnp.arange(N_ATTN_HEADS, dtype=F32) + 1.0
    return jnp.exp2(-8.0 * i / N_ATTN_HEADS)


def kernel(x, c, l0_ada_w, l0_ada_b, l0_w_in, l0_lam_q1, l0_lam_k1, l0_lam_q2, l0_lam_k2, l0_subln_w, l0_w_out, l0_ln1_g, l0_ln1_b, l0_ffn_w_gate, l0_ffn_w_up, l0_ffn_w_down, l0_ln2_g, l0_ln2_b, l1_ada_w, l1_ada_b, l1_a_re, l1_a_im, l1_log_dt, l1_b_re, l1_b_im, l1_c_re, l1_c_im, l1_d_skip, l1_w_glu, l1_b_glu, l1_ln1_g, l1_ln1_b, l1_router_w, l1_router_b, l1_exp_w_gate, l1_exp_w_up, l1_exp_w_down, l1_ln2_g, l1_ln2_b):
    mod0 = _ada(c, l0_ada_w, l0_ada_b)
    mod1 = _ada(c, l1_ada_w, l1_ada_b)

    q_a = jnp.full((HALF_D,), DIFF_QK_DIM ** -0.5 * LOG2E, F32)
    q_b = jnp.full((HALF_D,), HEAD_DIM ** -0.5 * LOG2E, F32)
    ones = jnp.ones((D_MODEL,), F32)
    colscale = jnp.concatenate([q_a, ones, q_b, ones]).reshape(1, -1)
    dils = tuple(dil for _, dil in DIL_PAIRS)
    proj, *planes = _qkv(x, mod0, l0_w_in.astype(BF16), colscale, dils[1:])
    slopes = _alibi_slopes()
    lam_vecs = jnp.stack([l0_lam_q1, l0_lam_k1, l0_lam_q2, l0_lam_k2]).astype(F32)
    subw = jnp.tile(l0_subln_w.astype(F32), 2).reshape(LANES, 1)
    oa = _diff_attention(proj, slopes[:N_HEADS_DIFF], lam_vecs, subw, 0)
    dil_slopes = tuple(2.0 ** (-8.0 * (h + 1.0) / N_ATTN_HEADS) for h in range(N_HEADS_DIFF, N_ATTN_HEADS))
    assert dils[0] == 1
    dil_outs = [_dilated_branch(proj[:, None], 1, dil_slopes, 3)]
    dil_outs += [_dilated_branch(pln, dil, dil_slopes, 0) for pln, dil in zip(planes, dils[1:])]
    x = _attn_out(oa, dil_outs, dils, x, mod0, l0_w_out.astype(BF16), l0_ln1_g, l0_ln1_b)
    x, u = _ffn(x, mod0, l0_ffn_w_gate.astype(BF16), l0_ffn_w_up.astype(BF16), l0_ffn_w_down.astype(BF16),
                l0_ln2_g, l0_ln2_b, next_mod=mod1)

    y_ssm = _s5(u, l1_a_re, l1_a_im, l1_log_dt, l1_b_re, l1_b_im, l1_c_re, l1_c_im)
    x, cw = _glu(x, y_ssm, mod1, l1_d_skip, l1_w_glu.astype(BF16), l1_b_glu, l1_ln1_g, l1_ln1_b,
                 l1_router_w, l1_router_b)
    x = _ffn(x, mod1, l1_exp_w_gate.astype(BF16), l1_exp_w_up.astype(BF16), l1_exp_w_down.astype(BF16),
             l1_ln2_g, l1_ln2_b, cw=cw)
    return x
```

```python
import functools
import math

import jax
import jax.numpy as jnp
from jax import lax
from jax.experimental import pallas as pl
from jax.experimental.pallas import tpu as pltpu

F32 = jnp.float32
BF16 = jnp.bfloat16

D_MODEL = 1024
DEPTH = 2
HEAD_DIM = 64
N_HEADS_DIFF = D_MODEL // (2 * HEAD_DIM)
DIFF_QK_DIM = HEAD_DIM // 2
N_HEADS_DIL = D_MODEL // (2 * HEAD_DIM)
N_ATTN_HEADS = N_HEADS_DIFF + N_HEADS_DIL
DIL_PAIRS = ((128, 1), (512, 4), (2048, 16))
ATTN_BLOCK = 128
SSM_GROUP_WIDTH = 16
SSM_GROUPS = D_MODEL // SSM_GROUP_WIDTH
SSM_STATE = 64
SSM_SUB = 16
D_FF = 2816
N_EXPERTS = 8
D_FF_EXPERT = 1408
DN_ALPHA = (2.0 * DEPTH) ** 0.25
EPS = 1e-5
LOG2E = 1.4426950408889634
LANES = 128
HALF_D = D_MODEL // 2
NEG_INF = float("-inf")

_NT = (((1,), (1,)), ((), ()))
_HI = lax.Precision.HIGHEST


def _params(sem, vmem_mb=48):
    return pltpu.CompilerParams(dimension_semantics=sem, vmem_limit_bytes=vmem_mb * 1024 * 1024)


def _layernorm(z, g, b):
    mu = jnp.mean(z, axis=-1, keepdims=True)
    zc = z - mu
    var = jnp.mean(zc * zc, axis=-1, keepdims=True)
    return zc * lax.rsqrt(var + EPS) * g + b


def _ada_kernel(c_ref, w_ref, b_ref, o_ref):
    h = jax.nn.silu(c_ref[...])
    o_ref[...] = jnp.dot(h.astype(BF16), w_ref[...].astype(BF16),
                         preferred_element_type=F32) + b_ref[...]


def _ada(c, w, b):
    bsz = c.shape[0]
    n = w.shape[1]
    tn = 1024
    out = pl.pallas_call(
        _ada_kernel,
        grid=(n // tn,),
        in_specs=[pl.BlockSpec((bsz, D_MODEL), lambda j: (0, 0)),
                  pl.BlockSpec((D_MODEL, tn), lambda j: (0, j)),
                  pl.BlockSpec((1, tn), lambda j: (0, j))],
        out_specs=pl.BlockSpec((bsz, tn), lambda j: (0, j)),
        out_shape=jax.ShapeDtypeStruct((bsz, n), F32),
        compiler_params=_params(("arbitrary",)),
        name="ada",
    )(c, w, b.reshape(1, n))
    return out.reshape(bsz, 6, D_MODEL)


def _qkv_kernel(x_ref, mod_ref, w_ref, cs_ref, o_ref, *rest, dils):
    perm_refs, stage_ref = rest[:-1], rest[-1]
    tm = x_ref.shape[1]
    mod = mod_ref[0]
    h = (x_ref[0] * (1.0 + mod[1:2]) + mod[0:1]).astype(BF16)
    for j in range(6):
        sl = slice(j * HALF_D, (j + 1) * HALF_D)
        acc = jnp.dot(h, w_ref[:, sl], preferred_element_type=F32) * cs_ref[:, sl]
        o_ref[0, :, sl] = acc.astype(BF16)
        if j >= 3 and dils:
            for c in range(HALF_D // LANES):
                stage_ref[(j - 3) * (HALF_D // LANES) + c] = acc[:, c * LANES:(c + 1) * LANES]
    for dil, p_ref in zip(dils, perm_refs):
        for r in range(dil):
            for c in range(stage_ref.shape[0]):
                rows = stage_ref[c, pl.ds(r, tm // dil, stride=dil), :]
                p_ref[0, r, :, c * LANES:(c + 1) * LANES] = rows.astype(BF16)


def _qkv(x, mod, w, colscale, dils, tm=512):
    bsz, s_len, _ = x.shape
    n = w.shape[1]
    nb = n // 2
    out_specs = [pl.BlockSpec((1, tm, n), lambda b, i: (b, i, 0))]
    out_shape = [jax.ShapeDtypeStruct((bsz, s_len, n), BF16)]
    for dil in dils:
        out_specs.append(pl.BlockSpec((1, dil, tm // dil, nb), lambda b, i: (b, 0, i, 0)))
        out_shape.append(jax.ShapeDtypeStruct((bsz, dil, s_len // dil, nb), BF16))
    return pl.pallas_call(
        functools.partial(_qkv_kernel, dils=dils),
        grid=(bsz, s_len // tm),
        in_specs=[pl.BlockSpec((1, tm, D_MODEL), lambda b, i: (b, i, 0)),
                  pl.BlockSpec((1, 6, D_MODEL), lambda b, i: (b, 0, 0)),
                  pl.BlockSpec((D_MODEL, n), lambda b, i: (0, 0)),
                  pl.BlockSpec((1, n), lambda b, i: (0, 0))],
        out_specs=out_specs,
        out_shape=out_shape,
        scratch_shapes=[pltpu.VMEM((nb // LANES, tm, LANES), F32)],
        compiler_params=_params(("parallel", "parallel")),
        name="qkv",
    )(x, mod, w, colscale)


def _diff_kernel(slope_ref, lam_ref, subw_ref, q_ref, k_ref, v_ref, o_ref,
                 acc_ref, m_ref, mask_ref, vat_ref, sa_ref, sb_ref, *, tk, lam_init):
    tq = 2 * tk
    pr = pl.program_id(1)
    qi = pl.program_id(2)
    n_kb = k_ref.shape[1] // tk
    sl2 = [slope_ref[2 * pr + hh] * LOG2E for hh in range(2)]

    @pl.when(qi == 0)
    def _():
        kr = lax.broadcasted_iota(jnp.int32, (tk, tq), 0)
        qc = lax.broadcasted_iota(jnp.int32, (tk, tq), 1)
        mask_ref[0] = jnp.where(qc >= kr, 0.0, NEG_INF)
        mask_ref[1] = jnp.where(qc - tk >= kr, 0.0, NEG_INF)
        row = lax.broadcasted_iota(jnp.int32, (LANES, tk), 0)
        for j in range(n_kb):
            vt = v_ref[0, j * tk:(j + 1) * tk, :].astype(F32).T
            vat_ref[0, j] = jnp.where(row < HEAD_DIM, vt, 1.0).astype(BF16)
            vat_ref[1, j] = jnp.where(row >= HEAD_DIM, vt, 1.0).astype(BF16)

    lane = lax.broadcasted_iota(jnp.int32, (1, LANES), 1)
    q = q_ref[0]
    feats = []
    for hh in range(2):
        slv = jnp.zeros((1, LANES), F32) + sl2[hh]
        hi = slv.astype(BF16).astype(F32)
        r1 = slv - hi
        lo = r1.astype(BF16).astype(F32)
        lo2 = (r1 - lo).astype(BF16).astype(F32)
        feat = jnp.where(lane == 0, hi, jnp.where(lane == 1, lo, jnp.where(lane == 2, lo2, 0.0)))
        feats.append(jnp.broadcast_to(feat, (tq, LANES)).astype(BF16))
    q_aug = [jnp.concatenate([jnp.where(lane // DIFF_QK_DIM == m, q, jnp.zeros_like(q)), feats[m // 2]], axis=1)
             for m in range(4)]
    key_pos = lax.broadcasted_iota(jnp.int32, (tk, LANES), 0).astype(F32)
    pos_feat = jnp.where(lane < 3, key_pos, 0.0).astype(BF16)
    acc_ref[...] = jnp.zeros_like(acc_ref)
    m_ref[...] = jnp.full_like(m_ref, NEG_INF)

    def scores(j, s_ref):
        off = pl.multiple_of(j * tk, tk)
        k_aug = jnp.concatenate([k_ref[0, pl.ds(off, tk), :], pos_feat], axis=1)
        for m in range(4):
            s_ref[m] = lax.dot_general(k_aug, q_aug[m], _NT, preferred_element_type=F32)

    def softmax_pv(j, s_ref, variant):
        blk_off = (qi * tq - j * tk).astype(F32)
        ps, alphas = [], []
        for m in range(4):
            t = s_ref[m] if variant == 0 else s_ref[m] + mask_ref[variant - 1]
            c = -sl2[m // 2] * blk_off
            m_old = m_ref[m]
            m_new = jnp.maximum(m_old, jnp.max(t, axis=0, keepdims=True) + c)
            alphas.append(jnp.exp2(m_old - m_new))
            ps.append(jnp.exp2(t - (m_new - c)).astype(BF16))
            m_ref[m] = m_new
        for m in range(4):
            acc_ref[m] = alphas[m] * acc_ref[m] + jnp.dot(vat_ref[m // 2, j], ps[m], preferred_element_type=F32)

    scores(0, sa_ref)

    def body(i, carry):
        j = 2 * i
        scores(j + 1, sb_ref)
        softmax_pv(j, sa_ref, 0)
        scores(j + 2, sa_ref)
        softmax_pv(j + 1, sb_ref, 0)
        return carry

    def last_block(j):
        off = pl.multiple_of(j * tk, tk)
        k_aug = jnp.concatenate([k_ref[0, pl.ds(off, tk), :], pos_feat], axis=1)
        blk_off = (qi * tq - j * tk).astype(F32)
        for m in range(4):
            s = lax.dot_general(k_aug, q_aug[m][tk:], _NT, preferred_element_type=F32)
            t = s + mask_ref[1, :, tk:]
            c = -sl2[m // 2] * blk_off
            m_old = m_ref[m, :, tk:]
            m_new = jnp.maximum(m_old, jnp.max(t, axis=0, keepdims=True) + c)
            alpha = jnp.exp2(m_old - m_new)
            p = jnp.exp2(t - (m_new - c)).astype(BF16)
            acc_ref[m, :, tk:] = alpha * acc_ref[m, :, tk:] + jnp.dot(vat_ref[m // 2, j], p,
                                                                     preferred_element_type=F32)
            m_ref[m, :, tk:] = m_new

    lax.fori_loop(0, qi, body, 0)
    softmax_pv(2 * qi, sa_ref, 1)
    last_block(2 * qi + 1)

    lam_v = lam_ref[...]
    lam = (jnp.exp(jnp.sum(lam_v[0:1] * lam_v[1:2], axis=-1, keepdims=True))
           - jnp.exp(jnp.sum(lam_v[2:3] * lam_v[3:4], axis=-1, keepdims=True)) + lam_init)
    normed = []
    for hh in range(2):
        o_rows = slice(hh * HEAD_DIM, (hh + 1) * HEAD_DIM)
        l_row = slice((1 - hh) * HEAD_DIM, (1 - hh) * HEAD_DIM + 1)
        outs = []
        for mm in range(2):
            a = acc_ref[2 * hh + mm]
            outs.append(a[o_rows] / a[l_row])
        d = outs[0] - lam * outs[1]
        ms = jnp.mean(d * d, axis=0, keepdims=True)
        normed.append(d * lax.rsqrt(ms + EPS))
    y_t = jnp.concatenate(normed, axis=0) * subw_ref[...] * (1.0 - lam_init)
    o_ref[0] = y_t.T.astype(BF16)


def _diff_attention(proj, slopes, lam_vecs, subw, layer_idx, tk=256):
    bsz, s_len, _ = proj.shape
    tq = 2 * tk
    n_pairs = N_HEADS_DIFF // 2
    lam_init = 0.8 - 0.6 * math.exp(-0.3 * layer_idx)
    kern = functools.partial(_diff_kernel, tk=tk, lam_init=lam_init)
    return pl.pallas_call(
        kern,
        grid=(bsz, n_pairs, s_len // tq),
        in_specs=[pl.BlockSpec(memory_space=pltpu.SMEM),
                  pl.BlockSpec((4, DIFF_QK_DIM), lambda b, p, i: (0, 0)),
                  pl.BlockSpec((LANES, 1), lambda b, p, i: (0, 0)),
                  pl.BlockSpec((1, tq, LANES), lambda b, p, i: (b, i, p)),
                  pl.BlockSpec((1, s_len, LANES), lambda b, p, i: (b, 0, n_pairs + p)),
                  pl.BlockSpec((1, s_len, LANES), lambda b, p, i: (b, 0, 2 * n_pairs + p))],
        out_specs=pl.BlockSpec((1, tq, LANES), lambda b, p, i: (b, i, p)),
        out_shape=jax.ShapeDtypeStruct((bsz, s_len, HALF_D), BF16),
        scratch_shapes=[pltpu.VMEM((4, LANES, tq), F32),
                        pltpu.VMEM((4, 1, tq), F32),
                        pltpu.VMEM((2, tk, tq), F32),
                        pltpu.VMEM((2, s_len // tk, LANES, tk), BF16),
                        pltpu.VMEM((4, tk, tq), F32),
                        pltpu.VMEM((4, tk, tq), F32)],
        compiler_params=_params(("parallel", "parallel", "arbitrary")),
        name="diff_attn",
    )(slopes, lam_vecs, subw, proj, proj, proj)


def _dil_kernel(q_ref, kp_ref, kc_ref, vp_ref, vc_ref, o_ref, l_ref, bias_ref, *, dil, slopes):
    blk = ATTN_BLOCK
    n = pl.program_id(2)
    first = (pl.program_id(0) == 0) & (pl.program_id(1) == 0) & (n == 0)

    @pl.when(first)
    def _():
        qi = lax.broadcasted_iota(jnp.int32, (blk, 2 * blk), 0)
        kj = lax.broadcasted_iota(jnp.int32, (blk, 2 * blk), 1)
        dist = qi + blk - kj
        valid = (dist >= 0) & (dist <= blk)
        dist_f = (dist * dil).astype(F32)
        for h in range(N_HEADS_DIL):
            bias = (-slopes[h] * LOG2E) * dist_f
            bias_ref[h, 0] = jnp.where(valid, bias, NEG_INF)
            bias_ref[h, 1] = jnp.where(valid & (kj >= blk), bias, NEG_INF)

    first_sel = (n == 0).astype(jnp.int32)
    lane = lax.broadcasted_iota(jnp.int32, (1, LANES), 1)
    even = lane < HEAD_DIM
    one = jnp.ones((), BF16)
    for i in range(q_ref.shape[0] // blk):
        rows = slice(i * blk, (i + 1) * blk)
        before = slice((i - 1) * blk, i * blk)
        sel = first_sel if i == 0 else 0
        for p in range(N_HEADS_DIL // 2):
            sl = slice(p * LANES, (p + 1) * LANES)
            qp = q_ref[rows, sl]
            k_prev = kp_ref[:, sl] if i == 0 else kc_ref[before, sl]
            v_prev = vp_ref[:, sl] if i == 0 else vc_ref[before, sl]
            kcat = jnp.concatenate([k_prev, kc_ref[rows, sl]], axis=0)
            vcat = jnp.concatenate([v_prev, vc_ref[rows, sl]], axis=0)
            accs, ms = [], []
            for hh in range(2):
                keep = even if hh == 0 else jnp.logical_not(even)
                qh = jnp.where(keep, qp, jnp.zeros_like(qp))
                va = jnp.where(keep, vcat, one)
                s = lax.dot_general(qh, kcat, _NT, preferred_element_type=F32)
                t = s + bias_ref[2 * p + hh, sel]
                m = jnp.max(t, axis=-1, keepdims=True)
                pm = jnp.exp2(t - m)
                accs.append(jnp.dot(pm.astype(BF16), va, preferred_element_type=F32))
                ms.append(m)
            o_un = jnp.where(even, accs[0], accs[1])
            l_sum = pltpu.roll(jnp.where(even, accs[1], accs[0]), HEAD_DIM, 1)
            m_pair = jnp.where(even, ms[0], ms[1])
            o_ref[rows, sl] = o_un / l_sum
            l_ref[rows, sl] = m_pair + jnp.log2(l_sum)


_DIL_BLOCKS_PER_STEP = 8


def _dilated_branch(qkv_planes, dil, slopes, col0):
    bsz, _, sub_len, _ = qkv_planes.shape
    blk = ATTN_BLOCK
    nb = sub_len // blk
    qb = min(_DIL_BLOCKS_PER_STEP, nb)
    kern = functools.partial(_dil_kernel, dil=dil, slopes=slopes)

    def cur(col):
        return pl.BlockSpec((None, None, qb * blk, HALF_D), lambda b, r, n: (b, r, n, col0 + col))

    def prev(col):
        return pl.BlockSpec((None, None, blk, HALF_D),
                            lambda b, r, n: (b, r, jnp.maximum(n * qb - 1, 0), col0 + col))

    out_spec = pl.BlockSpec((None, None, qb * blk, HALF_D), lambda b, r, n: (b, r, n, 0))
    return pl.pallas_call(
        kern,
        grid=(bsz, dil, nb // qb),
        in_specs=[cur(0), prev(1), cur(1), prev(2), cur(2)],
        out_specs=[out_spec, out_spec],
        out_shape=[jax.ShapeDtypeStruct((bsz, dil, sub_len, HALF_D), F32)] * 2,
        scratch_shapes=[pltpu.VMEM((N_HEADS_DIL, 2, blk, 2 * blk), F32)],
        compiler_params=_params(("arbitrary", "arbitrary", "arbitrary")),
        name=f"dilated_d{dil}",
    )(*([qkv_planes] * 5))


def _attn_out_kernel(oa_ref, o1_ref, l1_ref, o2_ref, l2_ref, o3_ref, l3_ref, x_ref, mod_ref,
                     w_ref, g_ref, b_ref, out_ref, tok_ref, ob_ref, *, dils):
    tm = x_ref.shape[1]
    branch_refs = ((o1_ref, l1_ref), (o2_ref, l2_ref), (o3_ref, l3_ref))
    for c in range(HALF_D // LANES):
        cols = slice(c * LANES, (c + 1) * LANES)
        vals = []
        for g, (dil, pair) in enumerate(zip(dils, branch_refs)):
            for a, src_ref in enumerate(pair):
                if dil == 1:
                    vals.append(src_ref[0, 0, :, cols])
                else:
                    for r in range(dil):
                        tok_ref[g, a, pl.ds(r, tm // dil, stride=dil), :] = src_ref[0, r, :, cols]
                    vals.append(tok_ref[g, a])
        o1, l1, o2, l2, o3, l3 = vals
        mx = jnp.maximum(jnp.maximum(l1, l2), l3)
        e1, e2, e3 = jnp.exp2(l1 - mx), jnp.exp2(l2 - mx), jnp.exp2(l3 - mx)
        ob_ref[:, cols] = ((e1 * o1 + e2 * o2 + e3 * o3) / (e1 + e2 + e3)).astype(BF16)
    y = (jnp.dot(oa_ref[0], w_ref[:HALF_D], preferred_element_type=F32)
         + jnp.dot(ob_ref[...], w_ref[HALF_D:], preferred_element_type=F32))
    mod = mod_ref[0]
    z = DN_ALPHA * x_ref[0] + (1.0 + mod[2:3]) * y
    out_ref[0] = _layernorm(z, g_ref[...], b_ref[...])


def _attn_out(oa, dil_outs, dils, x, mod, w, ln_g, ln_b, tm=512):
    bsz, s_len, _ = x.shape
    half = pl.BlockSpec((1, tm, HALF_D), lambda b, i: (b, i, 0))
    full = pl.BlockSpec((1, tm, D_MODEL), lambda b, i: (b, i, 0))
    vec = pl.BlockSpec((1, D_MODEL), lambda b, i: (0, 0))
    plane_specs = []
    for dil in dils:
        spec = pl.BlockSpec((1, dil, tm // dil, HALF_D), lambda b, i: (b, 0, i, 0))
        plane_specs += [spec, spec]
    flat = [a for pair in dil_outs for a in pair]
    return pl.pallas_call(
        functools.partial(_attn_out_kernel, dils=dils),
        grid=(bsz, s_len // tm),
        in_specs=[half] + plane_specs + [full, pl.BlockSpec((1, 6, D_MODEL), lambda b, i: (b, 0, 0)),
                                         pl.BlockSpec((D_MODEL, D_MODEL), lambda b, i: (0, 0)), vec, vec],
        out_specs=full,
        out_shape=jax.ShapeDtypeStruct((bsz, s_len, D_MODEL), F32),
        scratch_shapes=[pltpu.VMEM((len(dils), 2, tm, LANES), F32), pltpu.VMEM((tm, HALF_D), BF16)],
        compiler_params=_params(("parallel", "parallel")),
        name="attn_out",
    )(oa, *flat, x, mod, w, ln_g.reshape(1, -1), ln_b.reshape(1, -1))


def _ffn_kernel(*refs, gated, emit_next):
    it = iter(refs)
    x_ref, mod_ref = next(it), next(it)
    cw_ref = next(it) if gated else None
    wg_ref, wu_ref, wd_ref, g_ref, b_ref = next(it), next(it), next(it), next(it), next(it)
    nmod_ref = next(it) if emit_next else None
    out_ref = next(it)
    nxt_ref = next(it) if emit_next else None
    h_ref, acc_ref = next(it), next(it)

    j = pl.program_id(2)
    mod = mod_ref[0]

    @pl.when(j == 0)
    def _():
        h_ref[...] = (x_ref[0] * (1.0 + mod[4:5]) + mod[3:4]).astype(BF16)
        acc_ref[...] = jnp.zeros_like(acc_ref)

    h = h_ref[...]
    wg = wg_ref.at[0] if gated else wg_ref
    wu = wu_ref.at[0] if gated else wu_ref
    wd = wd_ref.at[0] if gated else wd_ref
    width = wg.shape[1]
    y = None
    for lo in range(0, width, _FF_CHUNK):
        hi = min(lo + _FF_CHUNK, width)
        gate = jnp.dot(h, wg[:, lo:hi], preferred_element_type=F32)
        up = jnp.dot(h, wu[:, lo:hi], preferred_element_type=F32)
        act = (jax.nn.silu(gate) * up).astype(BF16)
        part = jnp.dot(act, wd[lo:hi, :], preferred_element_type=F32)
        y = part if y is None else y + part
    if gated:
        lane = lax.broadcasted_iota(jnp.int32, (1, LANES), 1)
        cw = jnp.sum(jnp.where(lane == j, cw_ref[0], 0.0), axis=-1, keepdims=True)
        y = cw * y
    acc_ref[...] += y

    @pl.when(j == pl.num_programs(2) - 1)
    def _():
        z = DN_ALPHA * x_ref[0] + (1.0 + mod[5:6]) * acc_ref[...]
        out = _layernorm(z, g_ref[...], b_ref[...])
        out_ref[0] = out
        if emit_next:
            nmod = nmod_ref[0]
            nxt_ref[0] = (out * (1.0 + nmod[1:2]) + nmod[0:1]).astype(BF16)


_FF_CHUNK = 768


def _ffn(x, mod, wg, wu, wd, ln_g, ln_b, *, cw=None, next_mod=None, tm=1024, tf=1408):
    bsz, s_len, _ = x.shape
    gated = cw is not None
    emit_next = next_mod is not None
    full = pl.BlockSpec((1, tm, D_MODEL), lambda b, i, j: (b, i, 0))
    modspec = pl.BlockSpec((1, 6, D_MODEL), lambda b, i, j: (b, 0, 0))
    vec = pl.BlockSpec((1, D_MODEL), lambda b, i, j: (0, 0))
    if gated:
        n_inner = wg.shape[0]
        w_in_spec = pl.BlockSpec((1, D_MODEL, wg.shape[2]), lambda b, i, j: (j, 0, 0))
        w_dn_spec = pl.BlockSpec((1, wd.shape[1], D_MODEL), lambda b, i, j: (j, 0, 0))
    else:
        n_inner = wg.shape[1] // tf
        w_in_spec = pl.BlockSpec((D_MODEL, tf), lambda b, i, j: (0, j))
        w_dn_spec = pl.BlockSpec((tf, D_MODEL), lambda b, i, j: (j, 0))
    in_specs = [full, modspec]
    args = [x, mod]
    if gated:
        in_specs.append(pl.BlockSpec((1, tm, LANES), lambda b, i, j: (b, i, 0)))
        args.append(cw)
    in_specs += [w_in_spec, w_in_spec, w_dn_spec, vec, vec]
    args += [wg, wu, wd, ln_g.reshape(1, -1), ln_b.reshape(1, -1)]
    out_specs = [full]
    out_shape = [jax.ShapeDtypeStruct((bsz, s_len, D_MODEL), F32)]
    if emit_next:
        in_specs.append(modspec)
        args.append(next_mod)
        out_specs.append(full)
        out_shape.append(jax.ShapeDtypeStruct((bsz, s_len, D_MODEL), BF16))
    res = pl.pallas_call(
        functools.partial(_ffn_kernel, gated=gated, emit_next=emit_next),
        grid=(bsz, s_len // tm, n_inner),
        in_specs=in_specs,
        out_specs=out_specs,
        out_shape=out_shape,
        scratch_shapes=[pltpu.VMEM((tm, D_MODEL), BF16), pltpu.VMEM((tm, D_MODEL), F32)],
        compiler_params=_params(("parallel", "parallel", "arbitrary"), vmem_mb=56),
        name="moe_ffn" if gated else "ffn",
    )(*args)
    return res if emit_next else res[0]


def _s5_kernel(ar_r_ref, ai_r_ref, ar_c_ref, ai_c_ref, ldt_ref, brep_re_ref, brep_im_ref,
               bt_re_ref, bt_im_ref, c_re_ref, c_im_ref, u_ref, y_ref, sx_re, sx_im, h_re, h_im):
    sub = SSM_SUB
    gw = SSM_GROUP_WIDTH
    n_lane = sub * gw
    dt = jnp.exp(ldt_ref[0])

    def discretise(ar, ai):
        mag = jnp.exp(dt * ar)
        abr, abi = mag * jnp.cos(dt * ai), mag * jnp.sin(dt * ai)
        den = ar * ar + ai * ai
        nr, ni = abr - 1.0, abi
        return abr, abi, (nr * ar + ni * ai) / den, (ni * ar - nr * ai) / den

    abr, abi, zr, zi = discretise(ar_r_ref[0], ai_r_ref[0])
    _, _, zrc, zic = discretise(ar_c_ref[0], ai_c_ref[0])
    bt_r, bt_i = bt_re_ref[0], bt_im_ref[0]
    bbt_r, bbt_i = zr * bt_r - zi * bt_i, zr * bt_i + zi * bt_r
    brep_r, brep_i = brep_re_ref[0], brep_im_ref[0]
    bbrep_r, bbrep_i = zrc * brep_r - zic * brep_i, zrc * brep_i + zic * brep_r

    pw = []
    pr, pi = jnp.ones_like(abr), jnp.zeros_like(abr)
    for _ in range(sub + 1):
        pw.append((pr, pi))
        pr, pi = pr * abr - pi * abi, pr * abi + pi * abr
    cr, ci = c_re_ref[0], c_im_ref[0]
    m_re = [cr * a - ci * b for a, b in pw]
    m_im = [cr * b + ci * a for a, b in pw]

    kk = (jnp.dot(jnp.concatenate(m_re[:sub], axis=0), bbrep_r, precision=_HI, preferred_element_type=F32)
          - jnp.dot(jnp.concatenate(m_im[:sub], axis=0), bbrep_i, precision=_HI, preferred_element_type=F32))
    lane_slot = lax.broadcasted_iota(jnp.int32, (n_lane, n_lane), 1) // gw
    tt = jnp.zeros((n_lane, n_lane), F32)
    for s in range(sub):
        if s == 0:
            shifted = kk
        else:
            shifted = jnp.concatenate([jnp.zeros((s * gw, n_lane), F32), kk[:n_lane - s * gw]], axis=0)
        tt = jnp.where(lane_slot == s, shifted, tt)

    w_st_r = jnp.concatenate([bbt_r * pw[sub - 1 - s][0] - bbt_i * pw[sub - 1 - s][1] for s in range(sub)], axis=0)
    w_st_i = jnp.concatenate([bbt_r * pw[sub - 1 - s][1] + bbt_i * pw[sub - 1 - s][0] for s in range(sub)], axis=0)
    w_out_r = jnp.concatenate(m_re[1:], axis=0)
    w_out_i = jnp.concatenate(m_im[1:], axis=0)

    u = u_ref[0]
    def pad_lanes(w):
        return jnp.concatenate([w, jnp.zeros_like(w)], axis=1)

    sx_re[...] = jnp.dot(u, pad_lanes(w_st_r).astype(BF16), preferred_element_type=F32)
    sx_im[...] = jnp.dot(u, pad_lanes(w_st_i).astype(BF16), preferred_element_type=F32)

    n_batch = 8
    unroll = 8
    nj = u.shape[0] // n_batch
    alr = jnp.broadcast_to(pad_lanes(pw[sub][0]), (n_batch, LANES))
    ali = jnp.broadcast_to(pad_lanes(pw[sub][1]), (n_batch, LANES))

    def body(i, carry):
        hr, hi = carry
        for k in range(unroll):
            rows_j = pl.ds(i * unroll + k, n_batch, stride=nj)
            h_re[rows_j, :] = hr
            h_im[rows_j, :] = hi
            sr, si = sx_re[rows_j, :], sx_im[rows_j, :]
            hr, hi = alr * hr - ali * hi + sr, alr * hi + ali * hr + si
        return hr, hi

    zero = jnp.zeros((n_batch, LANES), F32)
    lax.fori_loop(0, nj // unroll, body, (zero, zero))

    y = lax.dot_general(u, tt.astype(BF16), _NT, preferred_element_type=F32)
    y += lax.dot_general(h_re[...].astype(BF16), pad_lanes(w_out_r).astype(BF16), _NT, preferred_element_type=F32)
    y -= lax.dot_general(h_im[...].astype(BF16), pad_lanes(w_out_i).astype(BF16), _NT, preferred_element_type=F32)
    y_ref[0] = y


_GROUPS_PER_TILE = LANES // SSM_GROUP_WIDTH
_SLOTS_PER_TILE = LANES // SSM_GROUP_WIDTH


def _s5_pack_kernel(x_ref, o_ref, stage_ref):
    tm = x_ref.shape[1]
    nj = tm // SSM_SUB
    n_tiles = D_MODEL // LANES
    for c in range(n_tiles):
        stage_ref[c] = x_ref[0, :, c * LANES:(c + 1) * LANES].astype(F32)
    slot = lax.broadcasted_iota(jnp.int32, (nj, LANES), 1) // SSM_GROUP_WIDTH

    def per_tile(c, carry):
        xs = [stage_ref[c, pl.ds(s, nj, stride=SSM_SUB), :] for s in range(SSM_SUB)]
        for gl in range(_GROUPS_PER_TILE):
            for h in range(SSM_SUB // _SLOTS_PER_TILE):
                acc = jnp.zeros((nj, LANES), F32)
                for sm in range(_SLOTS_PER_TILE):
                    shift = (SSM_GROUP_WIDTH * (sm - gl)) % LANES
                    acc = jnp.where(slot == sm, pltpu.roll(xs[_SLOTS_PER_TILE * h + sm], shift, 1), acc)
                o_ref[_GROUPS_PER_TILE * c + gl, :, h * LANES:(h + 1) * LANES] = acc.astype(BF16)
        return carry

    lax.fori_loop(0, n_tiles, per_tile, 0)


def _s5_unpack_kernel(y_ref, o_ref, stage_ref):
    nj = y_ref.shape[1]
    n_tiles = D_MODEL // LANES
    grp = lax.broadcasted_iota(jnp.int32, (nj, LANES), 1) // SSM_GROUP_WIDTH

    def per_tile(c, carry):
        for h in range(SSM_SUB // _SLOTS_PER_TILE):
            srcs = [y_ref[_GROUPS_PER_TILE * c + gl, :, h * LANES:(h + 1) * LANES] for gl in range(_GROUPS_PER_TILE)]
            for sm in range(_SLOTS_PER_TILE):
                acc = jnp.zeros((nj, LANES), F32)
                for gl in range(_GROUPS_PER_TILE):
                    shift = (SSM_GROUP_WIDTH * (gl - sm)) % LANES
                    acc = jnp.where(grp == gl, pltpu.roll(srcs[gl], shift, 1), acc)
                stage_ref[c, pl.ds(_SLOTS_PER_TILE * h + sm, nj, stride=SSM_SUB), :] = acc
        return carry

    lax.fori_loop(0, n_tiles, per_tile, 0)
    for c in range(n_tiles):
        o_ref[0, :, c * LANES:(c + 1) * LANES] = stage_ref[c]


def _s5(u, a_re, a_im, log_dt, b_re, b_im, c_re, c_im, tm=512):
    bsz, s_len, _ = u.shape
    assert bsz == 8, "the chunk scan holds one batch per sublane"
    g, p, gw, sub = SSM_GROUPS, SSM_STATE, SSM_GROUP_WIDTH, SSM_SUB
    nj = s_len // sub
    rows = nj * bsz
    n_t = s_len // tm
    tok_spec = pl.BlockSpec((1, tm, D_MODEL), lambda b, i: (b, i, 0))
    grp_spec = pl.BlockSpec((g, tm // sub, sub * gw), lambda b, i: (0, b * n_t + i, 0))
    stage = pltpu.VMEM((D_MODEL // LANES, tm, LANES), F32)
    ur = pl.pallas_call(
        _s5_pack_kernel,
        grid=(bsz, n_t),
        in_specs=[tok_spec],
        out_specs=grp_spec,
        out_shape=jax.ShapeDtypeStruct((g, rows, sub * gw), BF16),
        scratch_shapes=[stage],
        compiler_params=_params(("parallel", "parallel")),
        name="s5_pack",
    )(u)
    grp = lambda *shape: pl.BlockSpec((1,) + shape, lambda i: (i,) + (0,) * len(shape))
    yr = pl.pallas_call(
        _s5_kernel,
        grid=(g,),
        in_specs=[grp(1, p), grp(1, p), grp(p, 1), grp(p, 1), grp(1, 1),
                  grp(p, sub * gw), grp(p, sub * gw), grp(gw, p), grp(gw, p), grp(gw, p), grp(gw, p),
                  grp(rows, sub * gw)],
        out_specs=grp(rows, sub * gw),
        out_shape=jax.ShapeDtypeStruct((g, rows, sub * gw), F32),
        scratch_shapes=[pltpu.VMEM((rows, LANES), F32)] * 4,
        compiler_params=_params(("parallel",)),
        name="s5",
    )(a_re.reshape(g, 1, p), a_im.reshape(g, 1, p), a_re.reshape(g, p, 1), a_im.reshape(g, p, 1),
      log_dt.reshape(g, 1, 1), jnp.tile(b_re, (1, 1, sub)), jnp.tile(b_im, (1, 1, sub)),
      jnp.swapaxes(b_re, 1, 2), jnp.swapaxes(b_im, 1, 2), c_re, c_im, ur)
    return pl.pallas_call(
        _s5_unpack_kernel,
        grid=(bsz, n_t),
        in_specs=[grp_spec],
        out_specs=tok_spec,
        out_shape=jax.ShapeDtypeStruct((bsz, s_len, D_MODEL), F32),
        scratch_shapes=[stage],
        compiler_params=_params(("parallel", "parallel")),
        name="s5_unpack",
    )(yr)


def _glu_kernel(x_ref, y_ref, mod_ref, dskip_ref, w_ref, bias_ref, g_ref, b_ref, rw_ref, rb_ref,
                out_ref, cw_ref):
    mod = mod_ref[0]
    x = x_ref[0]
    u = x * (1.0 + mod[1:2]) + mod[0:1]
    y = y_ref[0] + dskip_ref[...] * u
    act = jax.nn.gelu(y).astype(BF16)
    z = jnp.dot(act, w_ref[...], preferred_element_type=F32) + bias_ref[...]
    yy = z[:, :D_MODEL] * jax.nn.sigmoid(z[:, D_MODEL:])
    out = _layernorm(DN_ALPHA * x + (1.0 + mod[2:3]) * yy, g_ref[...], b_ref[...])
    out_ref[0] = out
    cw_ref[0] = _route(out * (1.0 + mod[4:5]) + mod[3:4], rw_ref[...], rb_ref[...])


def _glu(x, y_ssm, mod, d_skip, w, bias, ln_g, ln_b, rw, rb, tm=512):
    bsz, s_len, _ = x.shape
    full = pl.BlockSpec((1, tm, D_MODEL), lambda b, i: (b, i, 0))
    vec = pl.BlockSpec((1, D_MODEL), lambda b, i: (0, 0))
    rw_pad = jnp.zeros((D_MODEL, LANES), F32).at[:, :N_EXPERTS].set(rw)
    rb_pad = jnp.zeros((1, LANES), F32).at[0, :N_EXPERTS].set(rb)
    return pl.pallas_call(
        _glu_kernel,
        grid=(bsz, s_len // tm),
        in_specs=[full, full, pl.BlockSpec((1, 6, D_MODEL), lambda b, i: (b, 0, 0)), vec,
                  pl.BlockSpec((D_MODEL, 2 * D_MODEL), lambda b, i: (0, 0)),
                  pl.BlockSpec((1, 2 * D_MODEL), lambda b, i: (0, 0)), vec, vec,
                  pl.BlockSpec((D_MODEL, LANES), lambda b, i: (0, 0)),
                  pl.BlockSpec((1, LANES), lambda b, i: (0, 0))],
        out_specs=[full, pl.BlockSpec((1, tm, LANES), lambda b, i: (b, i, 0))],
        out_shape=[jax.ShapeDtypeStruct((bsz, s_len, D_MODEL), F32),
                   jax.ShapeDtypeStruct((bsz, s_len, LANES), F32)],
        compiler_params=_params(("parallel", "parallel")),
        name="glu",
    )(x, y_ssm, mod, d_skip.reshape(1, -1), w, bias.reshape(1, -1), ln_g.reshape(1, -1), ln_b.reshape(1, -1),
      rw_pad, rb_pad)


def _route(h, w, bias):
    h_hi = h.astype(BF16)
    h_lo = (h - h_hi.astype(F32)).astype(BF16)
    w_hi = w.astype(BF16)
    w_lo = (w - w_hi.astype(F32)).astype(BF16)
    logits = (jnp.dot(h_hi, w_hi, preferred_element_type=F32)
              + jnp.dot(h_hi, w_lo, preferred_element_type=F32)
              + jnp.dot(h_lo, w_hi, preferred_element_type=F32)) + bias
    lane = lax.broadcasted_iota(jnp.int32, logits.shape, 1).astype(F32)
    logits = jnp.where(lane < N_EXPERTS, logits, NEG_INF)
    v1 = jnp.max(logits, axis=-1, keepdims=True)
    i1 = jnp.min(jnp.where(logits == v1, lane, float(LANES)), axis=-1, keepdims=True)
    rest = jnp.where(lane == i1, NEG_INF, logits)
    v2 = jnp.max(rest, axis=-1, keepdims=True)
    i2 = jnp.min(jnp.where(rest == v2, lane, float(LANES)), axis=-1, keepdims=True)
    e = jnp.exp(v2 - v1)
    g1 = 1.0 / (1.0 + e)
    g2 = e / (1.0 + e)
    return jnp.where(lane == i1, g1, 0.0) + jnp.where(lane == i2, g2, 0.0)


def _alibi_slopes():
    i = jnp.arange(N_ATTN_HEADS, dtype=F32) + 1.0
    return jnp.exp2(-8.0 * i / N_ATTN_HEADS)


def kernel(x, c, l0_ada_w, l0_ada_b, l0_w_in, l0_lam_q1, l0_lam_k1, l0_lam_q2, l0_lam_k2, l0_subln_w, l0_w_out, l0_ln1_g, l0_ln1_b, l0_ffn_w_gate, l0_ffn_w_up, l0_ffn_w_down, l0_ln2_g, l0_ln2_b, l1_ada_w, l1_ada_b, l1_a_re, l1_a_im, l1_log_dt, l1_b_re, l1_b_im, l1_c_re, l1_c_im, l1_d_skip, l1_w_glu, l1_b_glu, l1_ln1_g, l1_ln1_b, l1_router_w, l1_router_b, l1_exp_w_gate, l1_exp_w_up, l1_exp_w_down, l1_ln2_g, l1_ln2_b):
    mod0 = _ada(c, l0_ada_w, l0_ada_b)
    mod1 = _ada(c, l1_ada_w, l1_ada_b)

    q_a = jnp.full((HALF_D,), DIFF_QK_DIM ** -0.5 * LOG2E, F32)
    q_b = jnp.full((HALF_D,), HEAD_DIM ** -0.5 * LOG2E, F32)
    ones = jnp.ones((D_MODEL,), F32)
    colscale = jnp.concatenate([q_a, ones, q_b, ones]).reshape(1, -1)
    dils = tuple(dil for _, dil in DIL_PAIRS)
    proj, *planes = _qkv(x, mod0, l0_w_in.astype(BF16), colscale, dils[1:])
    slopes = _alibi_slopes()
    lam_vecs = jnp.stack([l0_lam_q1, l0_lam_k1, l0_lam_q2, l0_lam_k2]).astype(F32)
    subw = jnp.tile(l0_subln_w.astype(F32), 2).reshape(LANES, 1)
    oa = _diff_attention(proj, slopes[:N_HEADS_DIFF], lam_vecs, subw, 0)
    dil_slopes = tuple(2.0 ** (-8.0 * (h + 1.0) / N_ATTN_HEADS) for h in range(N_HEADS_DIFF, N_ATTN_HEADS))
    assert dils[0] == 1
    dil_outs = [_dilated_branch(proj[:, None], 1, dil_slopes, 3)]
    dil_outs += [_dilated_branch(pln, dil, dil_slopes, 0) for pln, dil in zip(planes, dils[1:])]
    x = _attn_out(oa, dil_outs, dils, x, mod0, l0_w_out.astype(BF16), l0_ln1_g, l0_ln1_b)
    x, u = _ffn(x, mod0, l0_ffn_w_gate.astype(BF16), l0_ffn_w_up.astype(BF16), l0_ffn_w_down.astype(BF16),
                l0_ln2_g, l0_ln2_b, next_mod=mod1)

    y_ssm = _s5(u, l1_a_re, l1_a_im, l1_log_dt, l1_b_re, l1_b_im, l1_c_re, l1_c_im)
    x, cw = _glu(x, y_ssm, mod1, l1_d_skip, l1_w_glu.astype(BF16), l1_b_glu, l1_ln1_g, l1_ln1_b,
                 l1_router_w, l1_router_b)
    x = _ffn(x, mod1, l1_exp_w_gate.astype(BF16), l1_exp_w_up.astype(BF16), l1_exp_w_down.astype(BF16),
             l1_ln2_g, l1_ln2_b, cw=cw)
    return x
```
